```python
import math
import jax, jax.numpy as jnp
from jax import lax
import numpy as np

D_MODEL = 1024
BATCH = 16
SEQ = 4096
DEPTH = 4

MIX_WIDTH = D_MODEL
DSA_HEADS = D_MODEL // 128
DSA_HEAD_DIM = 64
IDX_HEADS = 8
IDX_DIM = 64
TOPK_MAX = 256
DSA_Q_BLOCK = 64
DIFF_HEADS = D_MODEL // 256
DIFF_HEAD_DIM = 64
DIFF_V_DIM = 2 * DIFF_HEAD_DIM
DIFF_Q_BLOCK = 128
NUM_BUCKETS = 32
MAX_DISTANCE = 128
D_FF = ((8 * D_MODEL // 3 + 255) // 256) * 256
EPS = 1e-6

IN_WIDTHS = (
    DSA_HEADS * DSA_HEAD_DIM,
    DSA_HEADS * DSA_HEAD_DIM,
    DSA_HEADS * DSA_HEAD_DIM,
    IDX_HEADS * IDX_DIM,
    IDX_DIM,
    IDX_HEADS,
    DIFF_HEADS * 2 * DIFF_HEAD_DIM,
    DIFF_HEADS * 2 * DIFF_HEAD_DIM,
    DIFF_HEADS * DIFF_V_DIM,
)
IN_COLS = sum(IN_WIDTHS)
OUT_IN = DSA_HEADS * DSA_HEAD_DIM + DIFF_HEADS * DIFF_V_DIM

kernel_name = "hybrid_dsa_diffattn_parallel_heads"


def rms_norm(x, g):
    xf = x.astype(jnp.float32)
    y = xf * lax.rsqrt(jnp.mean(xf * xf, axis=-1, keepdims=True) + EPS)
    return (y * g.astype(jnp.float32)).astype(x.dtype)


def rel_bucket(dist):
    n = jnp.maximum(dist, 0)
    max_exact = NUM_BUCKETS // 2
    nf = jnp.maximum(n, 1).astype(jnp.float32)
    large = max_exact + (jnp.log(nf / max_exact) / math.log(MAX_DISTANCE / max_exact)
                         * (NUM_BUCKETS - max_exact)).astype(jnp.int32)
    large = jnp.minimum(large, NUM_BUCKETS - 1)
    return jnp.where(n < max_exact, n, large)


def split_cols(p, widths):
    outs, off = [], 0
    for w in widths:
        outs.append(p[..., off:off + w])
        off += w
    return outs


def dsa_mixer(q, k, v, q_idx, k_idx, w_idx, bias_tab):
    B, S, H, Dh = q.shape
    n_sel = min(TOPK_MAX, S // 4)
    n_blk = S // DSA_Q_BLOCK
    key_pos = jnp.arange(S, dtype=jnp.int32)

    def block(i):
        start = i * DSA_Q_BLOCK
        qb = lax.dynamic_slice_in_dim(q, start, DSA_Q_BLOCK, axis=1)
        qib = lax.dynamic_slice_in_dim(q_idx, start, DSA_Q_BLOCK, axis=1)
        wb = lax.dynamic_slice_in_dim(w_idx, start, DSA_Q_BLOCK, axis=1)
        q_pos = start + jnp.arange(DSA_Q_BLOCK, dtype=jnp.int32)
        dots = jnp.einsum('bqhd,bsd->bqhs', qib, k_idx) * (IDX_DIM ** -0.5)
        score = jnp.einsum('bqh,bqhs->bqs', wb * (IDX_HEADS ** -0.5), jax.nn.relu(dots))
        causal = key_pos[None, :] <= q_pos[:, None]
        score = jnp.where(causal[None], score.astype(jnp.float32), -jnp.inf)
        _, sel = lax.top_k(score, n_sel)
        valid = sel <= q_pos[None, :, None]
        flat = sel.reshape(B, -1)
        ks = jnp.take_along_axis(k, flat[:, :, None, None], axis=1).reshape(B, DSA_Q_BLOCK, n_sel, H, Dh)
        vs = jnp.take_along_axis(v, flat[:, :, None, None], axis=1).reshape(B, DSA_Q_BLOCK, n_sel, H, Dh)
        logits = jnp.einsum('bqhd,bqkhd->bqhk', qb, ks).astype(jnp.float32) * (Dh ** -0.5)
        bias = bias_tab[rel_bucket(q_pos[None, :, None] - sel)]
        logits = logits + jnp.transpose(bias, (0, 1, 3, 2)).astype(jnp.float32)
        logits = jnp.where(valid[:, :, None, :], logits, -jnp.inf)
        p = jax.nn.softmax(logits, axis=-1).astype(v.dtype)
        return jnp.einsum('bqhk,bqkhd->bqhd', p, vs)

    out = lax.map(block, jnp.arange(n_blk, dtype=jnp.int32))
    return jnp.transpose(out, (1, 0, 2, 3, 4)).reshape(B, S, H * Dh)


def diff_mixer(q, k, v, lam, lam_init, subln, bias_tab):
    B, S, Hd, _, Dd = q.shape
    n_blk = S // DIFF_Q_BLOCK
    key_pos = jnp.arange(S, dtype=jnp.int32)
    lamf = lam.astype(jnp.float32)
    lam_full = (jnp.exp(jnp.sum(lamf[0] * lamf[1])) - jnp.exp(jnp.sum(lamf[2] * lamf[3]))
                + lam_init)

    def block(i):
        start = i * DIFF_Q_BLOCK
        qb = lax.dynamic_slice_in_dim(q, start, DIFF_Q_BLOCK, axis=1)
        q_pos = start + jnp.arange(DIFF_Q_BLOCK, dtype=jnp.int32)
        logits = jnp.einsum('bqhcd,bshcd->bhcqs', qb, k).astype(jnp.float32) * (Dd ** -0.5)
        dist = q_pos[:, None] - key_pos[None, :]
        bias = jnp.transpose(bias_tab[rel_bucket(dist)], (2, 0, 1))
        logits = logits + bias[None, :, None].astype(jnp.float32)
        logits = jnp.where((dist >= 0)[None, None, None], logits, -jnp.inf)
        p = jax.nn.softmax(logits, axis=-1)
        attn = (p[:, :, 0] - lam_full * p[:, :, 1]).astype(v.dtype)
        return jnp.einsum('bhqs,bshe->bqhe', attn, v)

    out = lax.map(block, jnp.arange(n_blk, dtype=jnp.int32))
    out = jnp.transpose(out, (1, 0, 2, 3, 4)).reshape(B, S, Hd, 2 * Dd)
    out = rms_norm(out, subln) * (1.0 - lam_init)
    return out.reshape(B, S, Hd * 2 * Dd)


def setup_inputs(seed: int = 0) -> dict:
    key = jax.random.key(seed)
    ks = jax.random.split(key, 14)
    f32 = jnp.float32
    nrm = lambda k, shape, s: jax.random.normal(k, shape, f32) * s
    return {
        "x": nrm(ks[0], (BATCH, SEQ, D_MODEL), 1.0),
        "attn_norm": 1.0 + nrm(ks[1], (DEPTH, D_MODEL), 0.05),
        "w_in": nrm(ks[2], (DEPTH, D_MODEL, IN_COLS), D_MODEL ** -0.5),
        "diff_lambda": nrm(ks[3], (DEPTH, 4, DIFF_HEAD_DIM), 0.1),
        "diff_subln": 1.0 + nrm(ks[4], (DEPTH, DIFF_V_DIM), 0.05),
        "w_out": nrm(ks[5], (DEPTH, OUT_IN, D_MODEL), OUT_IN ** -0.5),
        "ffn_norm": 1.0 + nrm(ks[6], (DEPTH, D_MODEL), 0.05),
        "w_gate": nrm(ks[7], (DEPTH, D_MODEL, D_FF), D_MODEL ** -0.5),
        "w_up": nrm(ks[8], (DEPTH, D_MODEL, D_FF), D_MODEL ** -0.5),
        "w_down": nrm(ks[9], (DEPTH, D_FF, D_MODEL), D_FF ** -0.5),
        "rel_bias": nrm(ks[10], (NUM_BUCKETS, DSA_HEADS + DIFF_HEADS), 0.5),
        "final_norm": 1.0 + nrm(ks[11], (D_MODEL,), 0.05),
    }


def reference(x, attn_norm, w_in, diff_lambda, diff_subln, w_out, ffn_norm,
              w_gate, w_up, w_down, rel_bias, final_norm):
    B, S, _ = x.shape
    bias_dsa = rel_bias[:, :DSA_HEADS]
    bias_diff = rel_bias[:, DSA_HEADS:]
    for l in range(DEPTH):
        lam_init = 0.8 - 0.6 * math.exp(-0.3 * l)
        h = rms_norm(x, attn_norm[l])
        proj = jnp.einsum('bsd,dc->bsc', h, w_in[l])
        (dq, dk, dv, iq, ik, iw, fq, fk, fv) = split_cols(proj, IN_WIDTHS)
        dsa_out = dsa_mixer(
            dq.reshape(B, S, DSA_HEADS, DSA_HEAD_DIM),
            dk.reshape(B, S, DSA_HEADS, DSA_HEAD_DIM),
            dv.reshape(B, S, DSA_HEADS, DSA_HEAD_DIM),
            iq.reshape(B, S, IDX_HEADS, IDX_DIM), ik, iw, bias_dsa)
        diff_out = diff_mixer(
            fq.reshape(B, S, DIFF_HEADS, 2, DIFF_HEAD_DIM),
            fk.reshape(B, S, DIFF_HEADS, 2, DIFF_HEAD_DIM),
            fv.reshape(B, S, DIFF_HEADS, DIFF_V_DIM),
            diff_lambda[l], lam_init, diff_subln[l], bias_diff)
        mixed = jnp.concatenate([dsa_out, diff_out], axis=-1)
        x = x + jnp.einsum('bsc,cd->bsd', mixed, w_out[l])
        h = rms_norm(x, ffn_norm[l])
        g = jnp.einsum('bsd,df->bsf', h, w_gate[l])
        u = jnp.einsum('bsd,df->bsf', h, w_up[l])
        x = x + jnp.einsum('bsf,fd->bsd', jax.nn.silu(g) * u, w_down[l])
    return rms_norm(x, final_norm)
```

```python
import functools
import math

import jax
import jax.numpy as jnp
from jax import lax
from jax.experimental import pallas as pl
from jax.experimental.pallas import tpu as pltpu

DSA_HEADS = 8
DSA_HEAD_DIM = 64
IDX_HEADS = 8
IDX_DIM = 64
TOPK_MAX = 256
DIFF_HEADS = 4
DIFF_HEAD_DIM = 64
DIFF_V_DIM = 2 * DIFF_HEAD_DIM
NUM_BUCKETS = 32
MAX_DISTANCE = 128
EPS = 1e-6

HEAD_COLS = 512
N_MAIN = 7 * HEAD_COLS
N_AUX = 128
ATT_BLOCK = 256
ROW_BLOCK = 512
FF_CHUNK = 256

MASK_SCORE = -3.0e38
SELECT_ALL = -1.0e38
MASK_LOGIT = -1.0e30
VMEM_LIMIT = 56 * 1024 * 1024

F32 = jnp.float32
BF16 = jnp.bfloat16
_NT = (((1,), (1,)), ((), ()))


def _dot_nt(a, b):
    return lax.dot_general(a, b, _NT, preferred_element_type=F32)


def _inproj_kernel(x_ref, g_ref, w_ref, main_ref, aux_ref):
    x = x_ref[...]
    h = x * lax.rsqrt(jnp.mean(x * x, axis=-1, keepdims=True) + EPS) * g_ref[...]
    h = h.astype(BF16)
    for c in range(0, N_MAIN, HEAD_COLS):
        main_ref[:, c:c + HEAD_COLS] = jnp.dot(
            h, w_ref[:, c:c + HEAD_COLS], preferred_element_type=F32).astype(BF16)
    aux_ref[...] = jnp.dot(h, w_ref[:, N_MAIN:], preferred_element_type=F32)


def _inproj(x2, g, w_all):
    t, d = x2.shape
    return pl.pallas_call(
        _inproj_kernel,
        grid=(t // ROW_BLOCK,),
        in_specs=[
            pl.BlockSpec((ROW_BLOCK, d), lambda i: (i, 0)),
            pl.BlockSpec((1, d), lambda i: (0, 0)),
            pl.BlockSpec((d, N_MAIN + N_AUX), lambda i: (0, 0)),
        ],
        out_specs=[
            pl.BlockSpec((ROW_BLOCK, N_MAIN), lambda i: (i, 0)),
            pl.BlockSpec((ROW_BLOCK, N_AUX), lambda i: (i, 0)),
        ],
        out_shape=[
            jax.ShapeDtypeStruct((t, N_MAIN), BF16),
            jax.ShapeDtypeStruct((t, N_AUX), F32),
        ],
        compiler_params=pltpu.CompilerParams(
            dimension_semantics=("arbitrary",), vmem_limit_bytes=VMEM_LIMIT),
        name="inproj",
    )(x2, g, w_all)


def _build_rel_bias(tab_ref, bias_near_ref, head0, n_heads):
    tq = ATT_BLOCK
    r = lax.broadcasted_iota(jnp.int32, (tq, tq), 0)
    c = lax.broadcasted_iota(jnp.int32, (tq, tq), 1)
    max_exact = NUM_BUCKETS // 2
    for d in range(2):
        n = jnp.maximum(r - c + d * tq, 0)
        nf = jnp.maximum(n, 1).astype(F32)
        large = max_exact + (jnp.log(nf / max_exact) / math.log(MAX_DISTANCE / max_exact)
                             * (NUM_BUCKETS - max_exact)).astype(jnp.int32)
        large = jnp.minimum(large, NUM_BUCKETS - 1)
        bucket = jnp.where(n < max_exact, n, large)
        for h in range(n_heads):
            bias_near_ref[d, h] = jnp.zeros((tq, tq), F32)

        def fill(b, carry, d=d, bucket=bucket):
            hit = bucket == b
            for h in range(n_heads):
                bias_near_ref[d, h] = jnp.where(hit, tab_ref[b, head0 + h], bias_near_ref[d, h])
            return carry

        lax.fori_loop(0, NUM_BUCKETS, fill, 0)


def _flash_step(m_ref, l_ref, acc_ref, idx, s, v):
    m_prev = m_ref[idx]
    m_new = jnp.maximum(m_prev, jnp.max(s, axis=-1, keepdims=True))
    alpha = jnp.exp(m_prev - m_new)
    p = jnp.exp(s - m_new)
    l_ref[idx] = alpha * l_ref[idx] + jnp.sum(p, axis=-1, keepdims=True)
    acc_ref[idx] = alpha * acc_ref[idx] + jnp.dot(p.astype(BF16), v, preferred_element_type=F32)
    m_ref[idx] = m_new


def _causal_tile():
    r = lax.broadcasted_iota(jnp.int32, (ATT_BLOCK, ATT_BLOCK), 0)
    c = lax.broadcasted_iota(jnp.int32, (ATT_BLOCK, ATT_BLOCK), 1)
    return c <= r


def _dsa_kernel(n_sel, tab_ref, iq_ref, aux_ref, ik_ref, q_ref, k_ref, v_ref, out_ref,
                sc_ref, bias_ref, tri_ref, m_ref, l_ref, acc_ref):
    tq = ATT_BLOCK
    i = pl.program_id(1)
    first = (pl.program_id(0) == 0) & (i == 0)

    @pl.when(first)
    def _():
        _build_rel_bias(tab_ref, bias_ref, 0, DSA_HEADS)
        r = lax.broadcasted_iota(jnp.int32, (tq, tq), 0)
        c = lax.broadcasted_iota(jnp.int32, (tq, tq), 1)
        tri_ref[...] = jnp.where(r < c, 1.0, 0.0).astype(BF16)

    causal = _causal_tile()
    nsel_f = jnp.float32(n_sel)

    def chunk(j):
        return pl.ds(pl.multiple_of(j * tq, tq), tq)

    iq = iq_ref[0]
    iq_h = [iq[:, h * IDX_DIM:(h + 1) * IDX_DIM] for h in range(IDX_HEADS)]
    w = aux_ref[0][:, IDX_DIM:IDX_DIM + IDX_HEADS] * (IDX_HEADS ** -0.5 * IDX_DIM ** -0.5)
    w_h = [w[:, h:h + 1] for h in range(IDX_HEADS)]

    def score_tile(j):
        kc = ik_ref[0, chunk(j), :]
        acc = jnp.zeros((tq, tq), F32)
        for h in range(IDX_HEADS):
            acc = acc + w_h[h] * jnp.maximum(_dot_nt(iq_h[h], kc), 0.0)
        return acc

    def score_body(j, carry):
        lo, hi = carry
        acc = score_tile(j)
        sc_ref[:, chunk(j)] = acc
        return (jnp.minimum(lo, jnp.min(acc, axis=-1, keepdims=True)),
                jnp.maximum(hi, jnp.max(acc, axis=-1, keepdims=True)))

    big = jnp.float32(3.0e38)
    lo, hi = lax.fori_loop(0, i, score_body,
                           (jnp.full((tq, 1), big, F32), jnp.full((tq, 1), -big, F32)))
    acc = score_tile(i)
    sc_ref[:, chunk(i)] = jnp.where(causal, acc, MASK_SCORE)
    lo = jnp.minimum(lo, jnp.min(jnp.where(causal, acc, big), axis=-1, keepdims=True))
    hi = jnp.maximum(hi, jnp.max(jnp.where(causal, acc, -big), axis=-1, keepdims=True))
    hi = hi + jnp.maximum(jnp.abs(hi), 1.0e-30)

    def count(pred_fn):
        def body(j, acc):
            ind = jnp.where(pred_fn(sc_ref[:, chunk(j)]), 1.0, 0.0)
            return acc + ind[:, :128] + ind[:, 128:]
        part = lax.fori_loop(0, i + 1, body, jnp.zeros((tq, 128), F32))
        return jnp.sum(part, axis=-1, keepdims=True)

    def min_at_least(t):
        def body(j, acc):
            s = sc_ref[:, chunk(j)]
            s = jnp.where(s >= t, s, big)
            return jnp.minimum(acc, jnp.minimum(s[:, :128], s[:, 128:]))
        part = lax.fori_loop(0, i + 1, body, jnp.full((tq, 128), big, F32))
        return jnp.min(part, axis=-1, keepdims=True)

    row = i * tq + lax.broadcasted_iota(jnp.int32, (tq, 1), 0)
    small = row < n_sel

    def bisect(st):
        lo, hi, thr, found = st
        mid = 0.5 * lo + 0.5 * hi
        cnt = count(lambda s: s >= mid)
        up = cnt >= nsel_f
        hit = cnt == nsel_f
        thr = jnp.where(hit & (found == 0.0), mid, thr)
        found = jnp.where(hit, 1.0, found)
        return (jnp.where(up, mid, lo), jnp.where(up, hi, mid), thr, found)

    def unresolved(flag):
        return jnp.min(flag) == 0.0

    def search_cond(st):
        it, _, _, _, found = st
        return (it < 40) & unresolved(found)

    def search_body(st):
        it, lo, hi, thr, found = st
        lo, hi, thr, found = bisect((lo, hi, thr, found))
        return (it + 1, lo, hi, thr, found)

    thr0 = jnp.full((tq, 1), SELECT_ALL, F32)
    _, lo, hi, thr, found = lax.while_loop(
        search_cond, search_body, (jnp.int32(0), lo, hi, thr0, jnp.where(small, 1.0, 0.0)))

    def tie_probe(lo):
        tau = min_at_least(lo)
        above = count(lambda s: s > tau)
        return tau, above

    def tie_cond(st):
        _, _, _, found, _, above = st
        return unresolved(jnp.where(above < nsel_f, 1.0, found))

    def tie_body(st):
        lo, hi, thr, found, _, _ = st
        st2 = lax.fori_loop(0, 8, lambda _, s: bisect(s), (lo, hi, thr, found))
        lo, hi, thr, found = st2
        tau, above = tie_probe(lo)
        return (lo, hi, thr, found, tau, above)

    any_tie = unresolved(found)

    @pl.when(jnp.logical_not(any_tie))
    def _():
        def body(j, carry):
            sc_ref[:, chunk(j)] = jnp.where(sc_ref[:, chunk(j)] >= thr, 0.0, MASK_LOGIT)
            return carry
        lax.fori_loop(0, i + 1, body, 0)

    @pl.when(any_tie)
    def _():
        tau0, above0 = tie_probe(lo)
        _, _, thr2, found2, tau, above = lax.while_loop(
            tie_cond, tie_body, (lo, hi, thr, found, tau0, above0))
        found2 = found2 > 0.0
        cut = jnp.where(found2, thr2, tau)
        need = jnp.where(found2, jnp.float32(2 * ATT_BLOCK + 2), nsel_f - above)

        def body(j, seen):
            s = sc_ref[:, chunk(j)]
            eq = s == cut
            eqf = jnp.where(eq, 1.0, 0.0)
            rank = seen + jnp.dot(eqf.astype(BF16), tri_ref[...], preferred_element_type=F32)
            keep = (s > cut) | (eq & (rank < need))
            sc_ref[:, chunk(j)] = jnp.where(keep, 0.0, MASK_LOGIT)
            return seen + jnp.where(found2, 0.0, jnp.sum(eqf, axis=-1, keepdims=True))
        lax.fori_loop(0, i + 1, body, jnp.zeros((tq, 1), F32))

    q = q_ref[0]
    q_h = [q[:, h * DSA_HEAD_DIM:(h + 1) * DSA_HEAD_DIM] for h in range(DSA_HEADS)]
    for h in range(DSA_HEADS):
        m_ref[h] = jnp.full((tq, 1), MASK_LOGIT, F32)
        l_ref[h] = jnp.zeros((tq, 1), F32)
        acc_ref[h] = jnp.zeros((tq, DSA_HEAD_DIM), F32)

    def attend(j, bias_fn):
        kc = k_ref[0, chunk(j), :]
        vc = v_ref[0, chunk(j), :]
        sel = sc_ref[:, chunk(j)]
        for h in range(DSA_HEADS):
            cols = slice(h * DSA_HEAD_DIM, (h + 1) * DSA_HEAD_DIM)
            s = _dot_nt(q_h[h], kc[:, cols]) + bias_fn(h) + sel
            _flash_step(m_ref, l_ref, acc_ref, h, s, vc[:, cols])

    def far_body(j, carry):
        attend(j, lambda h: tab_ref[NUM_BUCKETS - 1, h])
        return carry

    lax.fori_loop(0, i - 1, far_body, 0)

    @pl.when(i >= 1)
    def _():
        attend(i - 1, lambda h: bias_ref[1, h])

    attend(i, lambda h: bias_ref[0, h])

    for h in range(DSA_HEADS):
        out_ref[0, :, h * DSA_HEAD_DIM:(h + 1) * DSA_HEAD_DIM] = (
            acc_ref[h] / l_ref[h]).astype(out_ref.dtype)


def _dsa(main, aux, ik, tab, n_sel):
    b, s, _ = main.shape
    tq = ATT_BLOCK
    col = lambda c: (lambda bi, i: (bi, i, c))
    full = lambda c: (lambda bi, i: (bi, 0, c))
    return pl.pallas_call(
        functools.partial(_dsa_kernel, n_sel),
        grid=(b, s // tq),
        in_specs=[
            pl.BlockSpec(memory_space=pltpu.SMEM),
            pl.BlockSpec((1, tq, HEAD_COLS), col(3)),
            pl.BlockSpec((1, tq, N_AUX), col(0)),
            pl.BlockSpec((1, s, IDX_DIM), full(0)),
            pl.BlockSpec((1, tq, HEAD_COLS), col(0)),
            pl.BlockSpec((1, s, HEAD_COLS), full(1)),
            pl.BlockSpec((1, s, HEAD_COLS), full(2)),
        ],
        out_specs=pl.BlockSpec((1, tq, HEAD_COLS), col(0)),
        out_shape=jax.ShapeDtypeStruct((b, s, HEAD_COLS), BF16),
        scratch_shapes=[
            pltpu.VMEM((tq, s), F32),
            pltpu.VMEM((2, DSA_HEADS, tq, tq), F32),
            pltpu.VMEM((tq, tq), BF16),
            pltpu.VMEM((DSA_HEADS, tq, 1), F32),
            pltpu.VMEM((DSA_HEADS, tq, 1), F32),
            pltpu.VMEM((DSA_HEADS, tq, DSA_HEAD_DIM), F32),
        ],
        compiler_params=pltpu.CompilerParams(
            dimension_semantics=("arbitrary", "arbitrary"), vmem_limit_bytes=VMEM_LIMIT),
        name="dsa",
    )(tab, main, aux, ik, main, main, main)


def _diff_kernel(tab_ref, lam_ref, init_ref, subln_ref, q_ref, k_ref, v_ref, out_ref,
                 bias_ref, m_ref, l_ref, acc_ref):
    tq = ATT_BLOCK
    i = pl.program_id(1)
    n_maps = 2 * DIFF_HEADS

    @pl.when((pl.program_id(0) == 0) & (i == 0))
    def _():
        _build_rel_bias(tab_ref, bias_ref, DSA_HEADS, DIFF_HEADS)

    causal_bias = jnp.where(_causal_tile(), 0.0, MASK_LOGIT)

    def chunk(j):
        return pl.ds(pl.multiple_of(j * tq, tq), tq)

    q = q_ref[0]
    q_m = [q[:, m * DIFF_HEAD_DIM:(m + 1) * DIFF_HEAD_DIM] for m in range(n_maps)]
    for m in range(n_maps):
        m_ref[m] = jnp.full((tq, 1), MASK_LOGIT, F32)
        l_ref[m] = jnp.zeros((tq, 1), F32)
        acc_ref[m] = jnp.zeros((tq, DIFF_V_DIM), F32)

    def attend(j, bias_fn):
        kc = k_ref[0, chunk(j), :]
        vc = v_ref[0, chunk(j), :]
        for h in range(DIFF_HEADS):
            bias = bias_fn(h)
            vh = vc[:, h * DIFF_V_DIM:(h + 1) * DIFF_V_DIM]
            for c in range(2):
                m = 2 * h + c
                s = _dot_nt(q_m[m], kc[:, m * DIFF_HEAD_DIM:(m + 1) * DIFF_HEAD_DIM]) + bias
                _flash_step(m_ref, l_ref, acc_ref, m, s, vh)

    def far_body(j, carry):
        attend(j, lambda h: tab_ref[NUM_BUCKETS - 1, DSA_HEADS + h])
        return carry

    lax.fori_loop(0, i - 1, far_body, 0)

    @pl.when(i >= 1)
    def _():
        attend(i - 1, lambda h: bias_ref[1, h])

    attend(i, lambda h: bias_ref[0, h] + causal_bias)

    lam = lam_ref[...]
    lam_init = init_ref[0:1, 0:1]
    lam_full = (jnp.exp(jnp.sum(lam[0:1] * lam[1:2], axis=-1, keepdims=True))
                - jnp.exp(jnp.sum(lam[2:3] * lam[3:4], axis=-1, keepdims=True)) + lam_init)
    for h in range(DIFF_HEADS):
        o = acc_ref[2 * h] / l_ref[2 * h] - lam_full * (acc_ref[2 * h + 1] / l_ref[2 * h + 1])
        o = o * lax.rsqrt(jnp.mean(o * o, axis=-1, keepdims=True) + EPS) * subln_ref[...]
        o = o * (1.0 - lam_init)
        out_ref[0, :, h * DIFF_V_DIM:(h + 1) * DIFF_V_DIM] = o.astype(out_ref.dtype)


def _diff(main, tab, lam, init_row, subln):
    b, s, _ = main.shape
    tq = ATT_BLOCK
    col = lambda c: (lambda bi, i: (bi, i, c))
    full = lambda c: (lambda bi, i: (bi, 0, c))
    const = lambda bi, i: (0, 0)
    n_maps = 2 * DIFF_HEADS
    return pl.pallas_call(
        _diff_kernel,
        grid=(b, s // tq),
        in_specs=[
            pl.BlockSpec(memory_space=pltpu.SMEM),
            pl.BlockSpec((4, DIFF_HEAD_DIM), const),
            pl.BlockSpec((1, 128), const),
            pl.BlockSpec((1, DIFF_V_DIM), const),
            pl.BlockSpec((1, tq, HEAD_COLS), col(4)),
            pl.BlockSpec((1, s, HEAD_COLS), full(5)),
            pl.BlockSpec((1, s, HEAD_COLS), full(6)),
        ],
        out_specs=pl.BlockSpec((1, tq, HEAD_COLS), col(0)),
        out_shape=jax.ShapeDtypeStruct((b, s, HEAD_COLS), BF16),
        scratch_shapes=[
            pltpu.VMEM((2, DIFF_HEADS, tq, tq), F32),
            pltpu.VMEM((n_maps, tq, 1), F32),
            pltpu.VMEM((n_maps, tq, 1), F32),
            pltpu.VMEM((n_maps, tq, DIFF_V_DIM), F32),
        ],
        compiler_params=pltpu.CompilerParams(
            dimension_semantics=("arbitrary", "arbitrary"), vmem_limit_bytes=VMEM_LIMIT),
        name="diff",
    )(tab, lam, init_row, subln, main, main, main)


def _ffn_kernel(final, x_ref, a_ref, b_ref, wo_ref, g_ref, wg_ref, wu_ref, wd_ref, gf_ref,
                out_ref, h_ref, y_ref):
    half = wo_ref.shape[0] // 2
    x1 = (x_ref[...]
          + jnp.dot(a_ref[...], wo_ref[:half, :], preferred_element_type=F32)
          + jnp.dot(b_ref[...], wo_ref[half:, :], preferred_element_type=F32))
    h = x1 * lax.rsqrt(jnp.mean(x1 * x1, axis=-1, keepdims=True) + EPS) * g_ref[...]
    h_ref[...] = h.astype(BF16)
    y_ref[...] = x1
    d_ff = wg_ref.shape[1]

    def body(f, carry):
        cols = pl.ds(pl.multiple_of(f * FF_CHUNK, FF_CHUNK), FF_CHUNK)
        hb = h_ref[...]
        g = jnp.dot(hb, wg_ref[:, cols], preferred_element_type=F32)
        u = jnp.dot(hb, wu_ref[:, cols], preferred_element_type=F32)
        act = (g / (1.0 + jnp.exp(-g)) * u).astype(BF16)
        y_ref[...] += jnp.dot(act, wd_ref[cols, :], preferred_element_type=F32)
        return carry

    lax.fori_loop(0, d_ff // FF_CHUNK, body, 0)
    y = y_ref[...]
    if final:
        y = y * lax.rsqrt(jnp.mean(y * y, axis=-1, keepdims=True) + EPS) * gf_ref[...]
    out_ref[...] = y


def _ffn(x2, a, b, wo, g, wg, wu, wd, gf, final):
    t, d = x2.shape
    d_ff = wg.shape[1]
    row = lambda i: (i, 0)
    const = lambda i: (0, 0)
    return pl.pallas_call(
        functools.partial(_ffn_kernel, final),
        grid=(t // ROW_BLOCK,),
        in_specs=[
            pl.BlockSpec((ROW_BLOCK, d), row),
            pl.BlockSpec((ROW_BLOCK, HEAD_COLS), row),
            pl.BlockSpec((ROW_BLOCK, HEAD_COLS), row),
            pl.BlockSpec((2 * HEAD_COLS, d), const),
            pl.BlockSpec((1, d), const),
            pl.BlockSpec((d, d_ff), const),
            pl.BlockSpec((d, d_ff), const),
            pl.BlockSpec((d_ff, d), const),
            pl.BlockSpec((1, d), const),
        ],
        out_specs=pl.BlockSpec((ROW_BLOCK, d), row),
        out_shape=jax.ShapeDtypeStruct((t, d), F32),
        scratch_shapes=[
            pltpu.VMEM((ROW_BLOCK, d), BF16),
            pltpu.VMEM((ROW_BLOCK, d), F32),
        ],
        compiler_params=pltpu.CompilerParams(
            dimension_semantics=("arbitrary",), vmem_limit_bytes=VMEM_LIMIT),
        name="ffn",
    )(x2, a, b, wo, g, wg, wu, wd, gf)


def _pack_w_in(w):
    hd = DSA_HEADS * DSA_HEAD_DIM
    off = 0
    parts = {}
    for name, width in (("dq", hd), ("dk", hd), ("dv", hd), ("iq", IDX_HEADS * IDX_DIM),
                        ("ik", IDX_DIM), ("iw", IDX_HEADS),
                        ("fq", 2 * DIFF_HEADS * DIFF_HEAD_DIM),
                        ("fk", 2 * DIFF_HEADS * DIFF_HEAD_DIM),
                        ("fv", DIFF_HEADS * DIFF_V_DIM)):
        parts[name] = w[:, off:off + width]
        off += width
    pad = jnp.zeros((w.shape[0], N_AUX - IDX_DIM - IDX_HEADS), w.dtype)
    return jnp.concatenate(
        [parts["dq"] * DSA_HEAD_DIM ** -0.5, parts["dk"], parts["dv"], parts["iq"],
         parts["fq"] * DIFF_HEAD_DIM ** -0.5, parts["fk"], parts["fv"],
         parts["ik"], parts["iw"], pad], axis=1).astype(BF16)


def kernel(x, attn_norm, w_in, diff_lambda, diff_subln, w_out, ffn_norm,
           w_gate, w_up, w_down, rel_bias, final_norm):
    b, s, d = x.shape
    depth = w_in.shape[0]
    n_sel = min(TOPK_MAX, s // 4)
    x2 = x.reshape(b * s, d)
    for l in range(depth):
        lam_init = 0.8 - 0.6 * math.exp(-0.3 * l)
        main, aux = _inproj(x2, attn_norm[l][None, :], _pack_w_in(w_in[l]))
        main = main.reshape(b, s, N_MAIN)
        aux = aux.reshape(b, s, N_AUX)
        ik = aux[:, :, :IDX_DIM].astype(BF16)
        dsa_out = _dsa(main, aux, ik, rel_bias, n_sel)
        diff_out = _diff(main, rel_bias, diff_lambda[l],
                         jnp.full((1, 128), lam_init, F32), diff_subln[l][None, :])
        x2 = _ffn(x2, dsa_out.reshape(b * s, HEAD_COLS), diff_out.reshape(b * s, HEAD_COLS),
                  w_out[l].astype(BF16), ffn_norm[l][None, :], w_gate[l].astype(BF16),
                  w_up[l].astype(BF16), w_down[l].astype(BF16), final_norm[None, :],
                  final=(l == depth - 1))
    return x2.reshape(b, s, d)
```

```python
import functools
import math

import jax
import jax.numpy as jnp
from jax import lax
from jax.experimental import pallas as pl
from jax.experimental.pallas import tpu as pltpu

DSA_HEADS = 8
DSA_HEAD_DIM = 64
IDX_HEADS = 8
IDX_DIM = 64
TOPK_MAX = 256
DIFF_HEADS = 4
DIFF_HEAD_DIM = 64
DIFF_V_DIM = 2 * DIFF_HEAD_DIM
NUM_BUCKETS = 32
MAX_DISTANCE = 128
EPS = 1e-6

HEAD_COLS = 512
N_MAIN = 7 * HEAD_COLS
N_AUX = 128
ATT_BLOCK = 256
ROW_BLOCK = 512
FF_CHUNK = 256

MASK_SCORE = -3.0e38
SELECT_ALL = -1.0e38
MASK_LOGIT = -1.0e30
VMEM_LIMIT = 56 * 1024 * 1024

F32 = jnp.float32
BF16 = jnp.bfloat16
_NT = (((1,), (1,)), ((), ()))


def _dot_nt(a, b):
    return lax.dot_general(a, b, _NT, preferred_element_type=F32)


def _inproj_kernel(x_ref, g_ref, w_ref, main_ref, aux_ref):
    x = x_ref[...]
    h = x * lax.rsqrt(jnp.mean(x * x, axis=-1, keepdims=True) + EPS) * g_ref[...]
    h = h.astype(BF16)
    for c in range(0, N_MAIN, HEAD_COLS):
        main_ref[:, c:c + HEAD_COLS] = jnp.dot(
            h, w_ref[:, c:c + HEAD_COLS], preferred_element_type=F32).astype(BF16)
    aux_ref[...] = jnp.dot(h, w_ref[:, N_MAIN:], preferred_element_type=F32)


def _inproj(x2, g, w_all):
    t, d = x2.shape
    return pl.pallas_call(
        _inproj_kernel,
        grid=(t // ROW_BLOCK,),
        in_specs=[
            pl.BlockSpec((ROW_BLOCK, d), lambda i: (i, 0)),
            pl.BlockSpec((1, d), lambda i: (0, 0)),
            pl.BlockSpec((d, N_MAIN + N_AUX), lambda i: (0, 0)),
        ],
        out_specs=[
            pl.BlockSpec((ROW_BLOCK, N_MAIN), lambda i: (i, 0)),
            pl.BlockSpec((ROW_BLOCK, N_AUX), lambda i: (i, 0)),
        ],
        out_shape=[
            jax.ShapeDtypeStruct((t, N_MAIN), BF16),
            jax.ShapeDtypeStruct((t, N_AUX), F32),
        ],
        compiler_params=pltpu.CompilerParams(
            dimension_semantics=("arbitrary",), vmem_limit_bytes=VMEM_LIMIT),
        name="inproj",
    )(x2, g, w_all)


def _build_rel_bias(tab_ref, bias_near_ref, head0, n_heads):
    t = ATT_BLOCK
    key = lax.broadcasted_iota(jnp.int32, (t, t), 0)
    query = lax.broadcasted_iota(jnp.int32, (t, t), 1)
    max_exact = NUM_BUCKETS // 2
    for d in range(2):
        n = jnp.maximum(query - key + d * t, 0)
        nf = jnp.maximum(n, 1).astype(F32)
        large = max_exact + (jnp.log(nf / max_exact) / math.log(MAX_DISTANCE / max_exact)
                             * (NUM_BUCKETS - max_exact)).astype(jnp.int32)
        large = jnp.minimum(large, NUM_BUCKETS - 1)
        bucket = jnp.where(n < max_exact, n, large)
        for h in range(n_heads):
            bias_near_ref[d, h] = jnp.zeros((t, t), F32)

        def fill(b, carry, d=d, bucket=bucket):
            hit = bucket == b
            for h in range(n_heads):
                val = tab_ref[b, head0 + h] - tab_ref[NUM_BUCKETS - 1, head0 + h]
                bias_near_ref[d, h] = jnp.where(hit, val, bias_near_ref[d, h])
            return carry

        lax.fori_loop(0, NUM_BUCKETS - 1, fill, 0)


def _chunk(j):
    return pl.ds(pl.multiple_of(j * ATT_BLOCK, ATT_BLOCK), ATT_BLOCK)


def _half_lane_mask(half):
    lane = lax.broadcasted_iota(jnp.int32, (ATT_BLOCK, 128), 1)
    return (lane >= 64 * half) & (lane < 64 * (half + 1))


def _flash_update(m_ref, l_ref, acc_ref, idx, s, vt):
    m = m_ref[idx]
    m_new = jnp.maximum(m, jnp.max(s, axis=0, keepdims=True))
    alpha = jnp.exp(m - m_new)
    p = jnp.exp(s - m_new)
    l_ref[idx] = alpha * l_ref[idx] + jnp.sum(p, axis=0, keepdims=True)
    acc_ref[idx] = alpha * acc_ref[idx] + jnp.dot(vt, p.astype(BF16), preferred_element_type=F32)
    m_ref[idx] = m_new


def _flash_init(m_ref, l_ref, acc_ref):
    m_ref[...] = jnp.full(m_ref.shape, MASK_LOGIT, F32)
    l_ref[...] = jnp.zeros(l_ref.shape, F32)
    acc_ref[...] = jnp.zeros(acc_ref.shape, F32)


def _dsa_kernel(n_sel, tab_ref, iq_ref, iwt_ref, ik_ref, q_ref, k_ref, vt_ref, out_ref,
                sc_ref, bias_ref, tri_ref, qz_ref, m_ref, l_ref, acc_ref, s_ref):
    t = ATT_BLOCK
    i = pl.program_id(1)

    @pl.when((pl.program_id(0) == 0) & (i == 0))
    def _():
        _build_rel_bias(tab_ref, bias_ref, 0, DSA_HEADS)
        r = lax.broadcasted_iota(jnp.int32, (t, t), 0)
        c = lax.broadcasted_iota(jnp.int32, (t, t), 1)
        tri_ref[...] = jnp.where(c < r, 1.0, 0.0).astype(BF16)

    key = lax.broadcasted_iota(jnp.int32, (t, t), 0)
    query = lax.broadcasted_iota(jnp.int32, (t, t), 1)
    causal = key <= query
    nsel_f = jnp.float32(n_sel)
    big = jnp.float32(3.0e38)

    def load_masked_heads(src_ref):
        x = src_ref[0]
        for h in range(DSA_HEADS):
            pair = x[:, 128 * (h // 2):128 * (h // 2 + 1)]
            qz_ref[h] = jnp.where(_half_lane_mask(h % 2), pair, jnp.zeros_like(pair))

    load_masked_heads(iq_ref)
    w = iwt_ref[0] * (IDX_HEADS ** -0.5 * IDX_DIM ** -0.5)

    def score_tile(j):
        kc = ik_ref[0, _chunk(j), :]
        acc = jnp.zeros((t, t), F32)
        for h in range(IDX_HEADS):
            acc = acc + w[h:h + 1, :] * jnp.maximum(_dot_nt(kc, qz_ref[h]), 0.0)
        return acc

    def score_body(j, carry):
        lo, hi = carry
        acc = score_tile(j)
        sc_ref[_chunk(j), :] = acc
        return (jnp.minimum(lo, jnp.min(acc, axis=0, keepdims=True)),
                jnp.maximum(hi, jnp.max(acc, axis=0, keepdims=True)))

    lo, hi = lax.fori_loop(0, i, score_body,
                           (jnp.full((1, t), big, F32), jnp.full((1, t), -big, F32)))
    acc = score_tile(i)
    sc_ref[_chunk(i), :] = jnp.where(causal, acc, MASK_SCORE)
    lo = jnp.minimum(lo, jnp.min(jnp.where(causal, acc, big), axis=0, keepdims=True))
    hi = jnp.maximum(hi, jnp.max(jnp.where(causal, acc, -big), axis=0, keepdims=True))
    hi = hi + jnp.maximum(jnp.abs(hi), 1.0e-30)

    def fold(x):
        return x.reshape(t // 8, 8, t)

    def count(pred_fn):
        def body(j, acc):
            ind = jnp.where(pred_fn(sc_ref[_chunk(j), :]), 1.0, 0.0)
            return acc + jnp.sum(fold(ind), axis=0)
        part = lax.fori_loop(0, i + 1, body, jnp.zeros((8, t), F32))
        return jnp.sum(part, axis=0, keepdims=True)

    def min_at_least(thr):
        def body(j, acc):
            s = sc_ref[_chunk(j), :]
            return jnp.minimum(acc, jnp.min(fold(jnp.where(s >= thr, s, big)), axis=0))
        part = lax.fori_loop(0, i + 1, body, jnp.full((8, t), big, F32))
        return jnp.min(part, axis=0, keepdims=True)

    pos = i * t + lax.broadcasted_iota(jnp.int32, (1, t), 1)
    small = pos < n_sel

    def bisect(st):
        lo, hi, thr, found = st
        mid = 0.5 * lo + 0.5 * hi
        cnt = count(lambda s: s >= mid)
        up = cnt >= nsel_f
        hit = cnt == nsel_f
        thr = jnp.where(hit & (found == 0.0), mid, thr)
        found = jnp.where(hit, 1.0, found)
        return (jnp.where(up, mid, lo), jnp.where(up, hi, mid), thr, found)

    def unresolved(flag):
        return jnp.min(flag) == 0.0

    def search_cond(st):
        it, _, _, _, found = st
        return (it < 40) & unresolved(found)

    def search_body(st):
        it, lo, hi, thr, found = st
        lo, hi, thr, found = bisect((lo, hi, thr, found))
        return (it + 1, lo, hi, thr, found)

    thr0 = jnp.full((1, t), SELECT_ALL, F32)
    _, lo, hi, thr, found = lax.while_loop(
        search_cond, search_body, (jnp.int32(0), lo, hi, thr0, jnp.where(small, 1.0, 0.0)))

    def tie_probe(lo):
        tau = min_at_least(lo)
        above = count(lambda s: s > tau)
        return tau, above

    def tie_cond(st):
        _, _, _, found, _, above = st
        return unresolved(jnp.where(above < nsel_f, 1.0, found))

    def tie_body(st):
        lo, hi, thr, found, _, _ = st
        lo, hi, thr, found = lax.fori_loop(0, 8, lambda _, s: bisect(s), (lo, hi, thr, found))
        tau, above = tie_probe(lo)
        return (lo, hi, thr, found, tau, above)

    any_tie = unresolved(found)

    @pl.when(jnp.logical_not(any_tie))
    def _():
        def body(j, carry):
            sc_ref[_chunk(j), :] = jnp.where(sc_ref[_chunk(j), :] >= thr, 0.0, MASK_LOGIT)
            return carry
        lax.fori_loop(0, i + 1, body, 0)

    @pl.when(any_tie)
    def _():
        tau0, above0 = tie_probe(lo)
        _, _, thr2, found2, tau, above = lax.while_loop(
            tie_cond, tie_body, (lo, hi, thr, found, tau0, above0))
        found2 = found2 > 0.0
        cut = jnp.where(found2, thr2, tau)
        need = jnp.where(found2, jnp.float32(2 * ATT_BLOCK + 2), nsel_f - above)

        def body(j, seen):
            s = sc_ref[_chunk(j), :]
            eq = s == cut
            eqf = jnp.where(eq, 1.0, 0.0)
            rank = seen + jnp.dot(tri_ref[...], eqf.astype(BF16), preferred_element_type=F32)
            keep = (s > cut) | (eq & (rank < need))
            sc_ref[_chunk(j), :] = jnp.where(keep, 0.0, MASK_LOGIT)
            return seen + jnp.where(found2, 0.0, jnp.sum(eqf, axis=0, keepdims=True))
        lax.fori_loop(0, i + 1, body, jnp.zeros((1, t), F32))

    load_masked_heads(q_ref)
    _flash_init(m_ref, l_ref, acc_ref)

    def step(j, carry, near=None):
        for h in range(DSA_HEADS):
            cols = slice(128 * (h // 2), 128 * (h // 2 + 1))
            s_ref[h] = _dot_nt(k_ref[0, _chunk(j), cols], qz_ref[h])
        for h in range(DSA_HEADS):
            rows = slice(h * DSA_HEAD_DIM, (h + 1) * DSA_HEAD_DIM)
            s = s_ref[h] + sc_ref[_chunk(j), :]
            if near is not None:
                s = s + bias_ref[near, h]
            _flash_update(m_ref, l_ref, acc_ref, h, s, vt_ref[0, rows, _chunk(j)])
        return carry

    lax.fori_loop(0, i - 1, step, 0)
    lax.fori_loop(jnp.maximum(i - 1, 0), i, functools.partial(step, near=1), 0)
    step(i, 0, near=0)

    for p in range(DSA_HEADS // 2):
        o = jnp.concatenate([acc_ref[h] / l_ref[h] for h in (2 * p, 2 * p + 1)], axis=0)
        out_ref[0, :, 128 * p:128 * (p + 1)] = o.T.astype(out_ref.dtype)


def _dsa(main, iwt, ik2, vt, tab, n_sel):
    b, s, _ = main.shape
    t = ATT_BLOCK
    col = lambda c: (lambda bi, i: (bi, i, c))
    full = lambda c: (lambda bi, i: (bi, 0, c))
    return pl.pallas_call(
        functools.partial(_dsa_kernel, n_sel),
        grid=(b, s // t),
        in_specs=[
            pl.BlockSpec(memory_space=pltpu.SMEM),
            pl.BlockSpec((1, t, HEAD_COLS), col(3)),
            pl.BlockSpec((1, IDX_HEADS, t), lambda bi, i: (bi, 0, i)),
            pl.BlockSpec((1, s, 2 * IDX_DIM), full(0)),
            pl.BlockSpec((1, t, HEAD_COLS), col(0)),
            pl.BlockSpec((1, s, HEAD_COLS), full(1)),
            pl.BlockSpec((1, HEAD_COLS, s), lambda bi, i: (bi, 0, 0)),
        ],
        out_specs=pl.BlockSpec((1, t, HEAD_COLS), col(0)),
        out_shape=jax.ShapeDtypeStruct((b, s, HEAD_COLS), BF16),
        scratch_shapes=[
            pltpu.VMEM((s, t), F32),
            pltpu.VMEM((2, DSA_HEADS, t, t), F32),
            pltpu.VMEM((t, t), BF16),
            pltpu.VMEM((DSA_HEADS, t, 128), BF16),
            pltpu.VMEM((DSA_HEADS, 1, t), F32),
            pltpu.VMEM((DSA_HEADS, 1, t), F32),
            pltpu.VMEM((DSA_HEADS, DSA_HEAD_DIM, t), F32),
            pltpu.VMEM((DSA_HEADS, t, t), F32),
        ],
        compiler_params=pltpu.CompilerParams(
            dimension_semantics=("arbitrary", "arbitrary"), vmem_limit_bytes=VMEM_LIMIT),
        name="dsa",
    )(tab, main, iwt, ik2, main, main, vt)


def _diff_kernel(tab_ref, lam_ref, init_ref, subln_ref, q_ref, k_ref, vt_ref, out_ref,
                 bias_ref, qz_ref, m_ref, l_ref, acc_ref, s_ref):
    t = ATT_BLOCK
    i = pl.program_id(1)
    n_maps = 2 * DIFF_HEADS

    @pl.when((pl.program_id(0) == 0) & (i == 0))
    def _():
        _build_rel_bias(tab_ref, bias_ref, DSA_HEADS, DIFF_HEADS)

    key = lax.broadcasted_iota(jnp.int32, (t, t), 0)
    query = lax.broadcasted_iota(jnp.int32, (t, t), 1)
    causal_bias = jnp.where(key <= query, 0.0, MASK_LOGIT)

    q = q_ref[0]
    for m in range(n_maps):
        pair = q[:, 128 * (m // 2):128 * (m // 2 + 1)]
        qz_ref[m] = jnp.where(_half_lane_mask(m % 2), pair, jnp.zeros_like(pair))

    lam = lam_ref[...]
    lam_init = init_ref[0:1, 0:1]
    lam_full = (jnp.exp(jnp.sum(lam[0:1] * lam[1:2], axis=-1, keepdims=True))
                - jnp.exp(jnp.sum(lam[2:3] * lam[3:4], axis=-1, keepdims=True)) + lam_init)

    _flash_init(m_ref, l_ref, acc_ref)

    def step(j, carry, near=None):
        for m in range(n_maps):
            cols = slice(128 * (m // 2), 128 * (m // 2 + 1))
            s_ref[m] = _dot_nt(k_ref[0, _chunk(j), cols], qz_ref[m])
        for m in range(n_maps):
            cols = slice(128 * (m // 2), 128 * (m // 2 + 1))
            s = s_ref[m]
            if near is not None:
                s = s + bias_ref[near, m // 2]
            if near == 0:
                s = s + causal_bias
            _flash_update(m_ref, l_ref, acc_ref, m, s, vt_ref[0, cols, _chunk(j)])
        return carry

    lax.fori_loop(0, i - 1, step, 0)
    lax.fori_loop(jnp.maximum(i - 1, 0), i, functools.partial(step, near=1), 0)
    step(i, 0, near=0)

    for h in range(DIFF_HEADS):
        cols = slice(128 * h, 128 * (h + 1))
        maps = [acc_ref[2 * h + c] / l_ref[2 * h + c] for c in range(2)]
        o = maps[0] - lam_full * maps[1]
        o = o * lax.rsqrt(jnp.mean(o * o, axis=0, keepdims=True) + EPS) * subln_ref[...]
        o = o * (1.0 - lam_init)
        out_ref[0, :, cols] = o.T.astype(out_ref.dtype)


def _diff(main, vt, tab, lam, init_row, subln_col):
    b, s, _ = main.shape
    t = ATT_BLOCK
    col = lambda c: (lambda bi, i: (bi, i, c))
    full = lambda c: (lambda bi, i: (bi, 0, c))
    const = lambda bi, i: (0, 0)
    return pl.pallas_call(
        _diff_kernel,
        grid=(b, s // t),
        in_specs=[
            pl.BlockSpec(memory_space=pltpu.SMEM),
            pl.BlockSpec((4, DIFF_HEAD_DIM), const),
            pl.BlockSpec((1, 128), const),
            pl.BlockSpec((DIFF_V_DIM, 1), const),
            pl.BlockSpec((1, t, HEAD_COLS), col(4)),
            pl.BlockSpec((1, s, HEAD_COLS), full(5)),
            pl.BlockSpec((1, HEAD_COLS, s), lambda bi, i: (bi, 0, 0)),
        ],
        out_specs=pl.BlockSpec((1, t, HEAD_COLS), col(0)),
        out_shape=jax.ShapeDtypeStruct((b, s, HEAD_COLS), BF16),
        scratch_shapes=[
            pltpu.VMEM((2, DIFF_HEADS, t, t), F32),
            pltpu.VMEM((2 * DIFF_HEADS, t, 128), BF16),
            pltpu.VMEM((2 * DIFF_HEADS, 1, t), F32),
            pltpu.VMEM((2 * DIFF_HEADS, 1, t), F32),
            pltpu.VMEM((2 * DIFF_HEADS, DIFF_V_DIM, t), F32),
            pltpu.VMEM((2 * DIFF_HEADS, t, t), F32),
        ],
        compiler_params=pltpu.CompilerParams(
            dimension_semantics=("arbitrary", "arbitrary"), vmem_limit_bytes=VMEM_LIMIT),
        name="diff",
    )(tab, lam, init_row, subln_col, main, main, vt)


def _ffn_kernel(final, x_ref, a_ref, b_ref, wo_ref, g_ref, wg_ref, wu_ref, wd_ref, gf_ref,
                out_ref, h_ref, y_ref):
    half = wo_ref.shape[0] // 2
    x1 = (x_ref[...]
          + jnp.dot(a_ref[...], wo_ref[:half, :], preferred_element_type=F32)
          + jnp.dot(b_ref[...], wo_ref[half:, :], preferred_element_type=F32))
    h = x1 * lax.rsqrt(jnp.mean(x1 * x1, axis=-1, keepdims=True) + EPS) * g_ref[...]
    h_ref[...] = h.astype(BF16)
    y_ref[...] = x1
    d_ff = wg_ref.shape[1]

    def body(f, carry):
        cols = pl.ds(pl.multiple_of(f * FF_CHUNK, FF_CHUNK), FF_CHUNK)
        hb = h_ref[...]
        g = jnp.dot(hb, wg_ref[:, cols], preferred_element_type=F32)
        u = jnp.dot(hb, wu_ref[:, cols], preferred_element_type=F32)
        act = (g / (1.0 + jnp.exp(-g)) * u).astype(BF16)
        y_ref[...] += jnp.dot(act, wd_ref[cols, :], preferred_element_type=F32)
        return carry

    lax.fori_loop(0, d_ff // FF_CHUNK, body, 0)
    y = y_ref[...]
    if final:
        y = y * lax.rsqrt(jnp.mean(y * y, axis=-1, keepdims=True) + EPS) * gf_ref[...]
    out_ref[...] = y


def _ffn(x2, a, b, wo, g, wg, wu, wd, gf, final):
    t, d = x2.shape
    d_ff = wg.shape[1]
    row = lambda i: (i, 0)
    const = lambda i: (0, 0)
    return pl.pallas_call(
        functools.partial(_ffn_kernel, final),
        grid=(t // ROW_BLOCK,),
        in_specs=[
            pl.BlockSpec((ROW_BLOCK, d), row),
            pl.BlockSpec((ROW_BLOCK, HEAD_COLS), row),
            pl.BlockSpec((ROW_BLOCK, HEAD_COLS), row),
            pl.BlockSpec((2 * HEAD_COLS, d), const),
            pl.BlockSpec((1, d), const),
            pl.BlockSpec((d, d_ff), const),
            pl.BlockSpec((d, d_ff), const),
            pl.BlockSpec((d_ff, d), const),
            pl.BlockSpec((1, d), const),
        ],
        out_specs=pl.BlockSpec((ROW_BLOCK, d), row),
        out_shape=jax.ShapeDtypeStruct((t, d), F32),
        scratch_shapes=[
            pltpu.VMEM((ROW_BLOCK, d), BF16),
            pltpu.VMEM((ROW_BLOCK, d), F32),
        ],
        compiler_params=pltpu.CompilerParams(
            dimension_semantics=("arbitrary",), vmem_limit_bytes=VMEM_LIMIT),
        name="ffn",
    )(x2, a, b, wo, g, wg, wu, wd, gf)


def _pack_w_in(w):
    hd = DSA_HEADS * DSA_HEAD_DIM
    off = 0
    parts = {}
    for name, width in (("dq", hd), ("dk", hd), ("dv", hd), ("iq", IDX_HEADS * IDX_DIM),
                        ("ik", IDX_DIM), ("iw", IDX_HEADS),
                        ("fq", 2 * DIFF_HEADS * DIFF_HEAD_DIM),
                        ("fk", 2 * DIFF_HEADS * DIFF_HEAD_DIM),
                        ("fv", DIFF_HEADS * DIFF_V_DIM)):
        parts[name] = w[:, off:off + width]
        off += width
    pad = jnp.zeros((w.shape[0], N_AUX - IDX_DIM - IDX_HEADS), w.dtype)
    return jnp.concatenate(
        [parts["dq"] * DSA_HEAD_DIM ** -0.5, parts["dk"], parts["dv"], parts["iq"],
         parts["fq"] * DIFF_HEAD_DIM ** -0.5, parts["fk"], parts["fv"],
         parts["ik"], parts["iw"], pad], axis=1).astype(BF16)


def kernel(x, attn_norm, w_in, diff_lambda, diff_subln, w_out, ffn_norm,
           w_gate, w_up, w_down, rel_bias, final_norm):
    b, s, d = x.shape
    depth = w_in.shape[0]
    n_sel = min(TOPK_MAX, s // 4)
    x2 = x.reshape(b * s, d)
    for l in range(depth):
        lam_init = 0.8 - 0.6 * math.exp(-0.3 * l)
        main, aux = _inproj(x2, attn_norm[l][None, :], _pack_w_in(w_in[l]))
        main = main.reshape(b, s, N_MAIN)
        aux = aux.reshape(b, s, N_AUX)
        ik = aux[:, :, :IDX_DIM].astype(BF16)
        ik2 = jnp.concatenate([ik, ik], axis=-1)
        iwt = jnp.swapaxes(aux[:, :, IDX_DIM:IDX_DIM + IDX_HEADS], 1, 2)
        dvt = jnp.swapaxes(main[:, :, 2 * HEAD_COLS:3 * HEAD_COLS], 1, 2)
        fvt = jnp.swapaxes(main[:, :, 6 * HEAD_COLS:7 * HEAD_COLS], 1, 2)
        dsa_out = _dsa(main, iwt, ik2, dvt, rel_bias, n_sel)
        diff_out = _diff(main, fvt, rel_bias, diff_lambda[l],
                         jnp.full((1, 128), lam_init, F32), diff_subln[l][:, None])
        x2 = _ffn(x2, dsa_out.reshape(b * s, HEAD_COLS), diff_out.reshape(b * s, HEAD_COLS),
                  w_out[l].astype(BF16), ffn_norm[l][None, :], w_gate[l].astype(BF16),
                  w_up[l].astype(BF16), w_down[l].astype(BF16), final_norm[None, :],
                  final=(l == depth - 1))
    return x2.reshape(b, s, d)
```

```python
import functools
import math

import jax
import jax.numpy as jnp
from jax import lax
from jax.experimental import pallas as pl
from jax.experimental.pallas import tpu as pltpu

DSA_HEADS = 8
DSA_HEAD_DIM = 64
IDX_HEADS = 8
IDX_DIM = 64
TOPK_MAX = 256
DIFF_HEADS = 4
DIFF_HEAD_DIM = 64
DIFF_V_DIM = 2 * DIFF_HEAD_DIM
NUM_BUCKETS = 32
MAX_DISTANCE = 128
EPS = 1e-6

HEAD_COLS = 512
N_MAIN = 7 * HEAD_COLS
N_AUX = 128
ATT_BLOCK = 256
ROW_BLOCK = 512
FF_CHUNK = 256

MASK_SCORE = -3.0e38
SELECT_ALL = -1.0e38
MASK_LOGIT = -1.0e30
LOG2E = math.log2(math.e)
SEARCH_FIRST_STEPS = 16
SEARCH_EXTRA_ROUNDS = 12
VMEM_LIMIT = 56 * 1024 * 1024

F32 = jnp.float32
BF16 = jnp.bfloat16

def _inproj_kernel(x_ref, g_ref, w_ref, main_ref, aux_ref):
    x = x_ref[...]
    h = x * lax.rsqrt(jnp.mean(x * x, axis=-1, keepdims=True) + EPS) * g_ref[...]
    h = h.astype(BF16)
    for c in range(0, N_MAIN, HEAD_COLS):
        main_ref[:, c:c + HEAD_COLS] = jnp.dot(
            h, w_ref[:, c:c + HEAD_COLS], preferred_element_type=F32).astype(BF16)
    aux_ref[...] = jnp.dot(h, w_ref[:, N_MAIN:], preferred_element_type=F32)


def _inproj(x2, g, w_all):
    t, d = x2.shape
    return pl.pallas_call(
        _inproj_kernel,
        grid=(t // ROW_BLOCK,),
        in_specs=[
            pl.BlockSpec((ROW_BLOCK, d), lambda i: (i, 0)),
            pl.BlockSpec((1, d), lambda i: (0, 0)),
            pl.BlockSpec((d, N_MAIN + N_AUX), lambda i: (0, 0)),
        ],
        out_specs=[
            pl.BlockSpec((ROW_BLOCK, N_MAIN), lambda i: (i, 0)),
            pl.BlockSpec((ROW_BLOCK, N_AUX), lambda i: (i, 0)),
        ],
        out_shape=[
            jax.ShapeDtypeStruct((t, N_MAIN), BF16),
            jax.ShapeDtypeStruct((t, N_AUX), F32),
        ],
        compiler_params=pltpu.CompilerParams(
            dimension_semantics=("arbitrary",), vmem_limit_bytes=VMEM_LIMIT),
        name="inproj",
    )(x2, g, w_all)


def _build_rel_bias(tab_ref, bias_near_ref, head0, n_heads):
    t = ATT_BLOCK
    key = lax.broadcasted_iota(jnp.int32, (t, t), 0)
    query = lax.broadcasted_iota(jnp.int32, (t, t), 1)
    max_exact = NUM_BUCKETS // 2
    for d in range(2):
        n = jnp.maximum(query - key + d * t, 0)
        nf = jnp.maximum(n, 1).astype(F32)
        large = max_exact + (jnp.log(nf / max_exact) / math.log(MAX_DISTANCE / max_exact)
                             * (NUM_BUCKETS - max_exact)).astype(jnp.int32)
        large = jnp.minimum(large, NUM_BUCKETS - 1)
        bucket = jnp.where(n < max_exact, n, large)
        for h in range(n_heads):
            bias_near_ref[d, h] = jnp.zeros((t, t), F32)

        def fill(b, carry, d=d, bucket=bucket):
            hit = bucket == b
            for h in range(n_heads):
                val = (tab_ref[b, head0 + h] - tab_ref[NUM_BUCKETS - 1, head0 + h]) * LOG2E
                bias_near_ref[d, h] = jnp.where(hit, val, bias_near_ref[d, h])
            return carry

        lax.fori_loop(0, NUM_BUCKETS - 1, fill, 0)


def _chunk(j):
    return pl.ds(pl.multiple_of(j * ATT_BLOCK, ATT_BLOCK), ATT_BLOCK)


def _half_lane_mask(half):
    lane = lax.broadcasted_iota(jnp.int32, (ATT_BLOCK, 128), 1)
    return (lane >= 64 * half) & (lane < 64 * (half + 1))


def _flash_update(m_ref, l_ref, acc_ref, idx, s, vt):
    m = m_ref[idx]
    m_new = jnp.maximum(m, jnp.max(s, axis=0, keepdims=True))
    alpha = jnp.exp2(m - m_new)
    p = jnp.exp2(s - m_new)
    l_ref[idx] = alpha * l_ref[idx] + jnp.sum(p, axis=0, keepdims=True)
    acc_ref[idx] = alpha * acc_ref[idx] + jnp.dot(vt, p.astype(BF16), preferred_element_type=F32)
    m_ref[idx] = m_new


def _flash_init(m_ref, l_ref, acc_ref):
    m_ref[...] = jnp.full(m_ref.shape, MASK_LOGIT, F32)
    l_ref[...] = jnp.zeros(l_ref.shape, F32)
    acc_ref[...] = jnp.zeros(acc_ref.shape, F32)


def _dsa_kernel(n_sel, tab_ref, iq_ref, iwt_ref, ik_ref, q_ref, k_ref, vt_ref, out_ref,
                sc_ref, bias_ref, tri_ref, qz_ref, m_ref, l_ref, acc_ref, s_ref):
    t = ATT_BLOCK
    i = pl.program_id(1)

    @pl.when((pl.program_id(0) == 0) & (i == 0))
    def _():
        _build_rel_bias(tab_ref, bias_ref, 0, DSA_HEADS)
        r = lax.broadcasted_iota(jnp.int32, (t, t), 0)
        c = lax.broadcasted_iota(jnp.int32, (t, t), 1)
        tri_ref[...] = jnp.where(c < r, 1.0, 0.0).astype(BF16)

    key = lax.broadcasted_iota(jnp.int32, (t, t), 0)
    query = lax.broadcasted_iota(jnp.int32, (t, t), 1)
    causal = key <= query
    nsel_f = jnp.float32(n_sel)
    big = jnp.float32(3.0e38)

    def load_masked_heads(src_ref):
        x = src_ref[0]
        for h in range(DSA_HEADS):
            pair = x[:, 128 * (h // 2):128 * (h // 2 + 1)]
            qz_ref[h] = jnp.where(_half_lane_mask(h % 2), pair, jnp.zeros_like(pair)).T

    load_masked_heads(iq_ref)
    w = iwt_ref[0] * (IDX_HEADS ** -0.5 * IDX_DIM ** -0.5)

    def score_tile(j):
        kc = ik_ref[0, _chunk(j), :]
        acc = jnp.zeros((t, t), F32)
        for h in range(IDX_HEADS):
            d = jnp.dot(kc, qz_ref[h], preferred_element_type=F32)
            acc = acc + w[h:h + 1, :] * jnp.maximum(d, 0.0)
        return acc

    def score_body(j, carry):
        lo, hi = carry
        acc = score_tile(j)
        sc_ref[_chunk(j), :] = acc
        return (jnp.minimum(lo, jnp.min(acc, axis=0, keepdims=True)),
                jnp.maximum(hi, jnp.max(acc, axis=0, keepdims=True)))

    lo, hi = lax.fori_loop(0, i, score_body,
                           (jnp.full((1, t), big, F32), jnp.full((1, t), -big, F32)))
    acc = score_tile(i)
    sc_ref[_chunk(i), :] = jnp.where(causal, acc, MASK_SCORE)
    lo = jnp.minimum(lo, jnp.min(jnp.where(causal, acc, big), axis=0, keepdims=True))
    hi = jnp.maximum(hi, jnp.max(jnp.where(causal, acc, -big), axis=0, keepdims=True))
    hi = hi + jnp.maximum(jnp.abs(hi), 1.0e-30)

    @pl.when(i % 2 == 0)
    def _():
        sc_ref[_chunk(i + 1), :] = jnp.full((t, t), MASK_SCORE, F32)

    n_span = i // 2 + 1

    def span(jj):
        return pl.ds(pl.multiple_of(jj * (2 * t), 2 * t), 2 * t)

    def fold(x):
        return x.reshape(x.shape[0] // 8, 8, t)

    def count(pred_fn):
        def body(jj, acc):
            for r in range(0, 2 * t, t // 2):
                rows = pl.ds(pl.multiple_of(jj * (2 * t) + r, t // 2), t // 2)
                ind = jnp.where(pred_fn(sc_ref[rows, :]), 1.0, 0.0)
                acc = acc + jnp.sum(fold(ind), axis=0)
            return acc
        part = lax.fori_loop(0, n_span, body, jnp.zeros((8, t), F32))
        return jnp.sum(part, axis=0, keepdims=True)

    def min_at_least(thr):
        def body(jj, acc):
            s = sc_ref[span(jj), :]
            return jnp.minimum(acc, jnp.min(fold(jnp.where(s >= thr, s, big)), axis=0))
        part = lax.fori_loop(0, n_span, body, jnp.full((8, t), big, F32))
        return jnp.min(part, axis=0, keepdims=True)

    pos = i * t + lax.broadcasted_iota(jnp.int32, (1, t), 1)
    small = pos < n_sel

    def count_pos_zero():
        def body(jj, acc):
            s = sc_ref[span(jj), :]
            return (acc[0] + jnp.sum(fold(jnp.where(s > 0.0, 1.0, 0.0)), axis=0),
                    acc[1] + jnp.sum(fold(jnp.where(s == 0.0, 1.0, 0.0)), axis=0))
        zero = jnp.zeros((8, t), F32)
        a, b = lax.fori_loop(0, n_span, body, (zero, zero))
        return jnp.sum(a, axis=0, keepdims=True), jnp.sum(b, axis=0, keepdims=True)

    n_pos, n_zero = count_pos_zero()
    zero_cut = (n_pos < nsel_f) & (n_pos + n_zero >= nsel_f) & jnp.logical_not(small)
    zflag = jnp.where(zero_cut, 1.0, 0.0)
    lo = jnp.where(n_pos >= nsel_f, 0.0, lo)
    hi = jnp.where(n_pos + n_zero < nsel_f, 0.0, hi)

    def bisect(st):
        lo, hi, thr, found = st
        mid = 0.5 * lo + 0.5 * hi
        cnt = count(lambda s: s >= mid)
        up = cnt >= nsel_f
        hit = cnt == nsel_f
        thr = jnp.where(hit & (found == 0.0), mid, thr)
        found = jnp.where(hit, 1.0, found)
        return (jnp.where(up, mid, lo), jnp.where(up, hi, mid), thr, found)

    def bisect_n(n, st):
        return lax.fori_loop(0, n, lambda _, s: bisect(s), st)

    def unresolved(flag):
        return jnp.min(flag) == 0.0

    def search_cond(st):
        it, _, _, _, found = st
        return (it < SEARCH_EXTRA_ROUNDS) & unresolved(jnp.maximum(found, zflag))

    def search_body(st):
        it, lo, hi, thr, found = st
        lo, hi, thr, found = bisect_n(2, (lo, hi, thr, found))
        return (it + 1, lo, hi, thr, found)

    thr0 = jnp.full((1, t), SELECT_ALL, F32)
    st = bisect_n(SEARCH_FIRST_STEPS, (lo, hi, thr0, jnp.where(small, 1.0, 0.0)))
    _, lo, hi, thr, found = lax.while_loop(search_cond, search_body, (jnp.int32(0),) + st)

    def tie_cond(st):
        _, _, _, found, _, above = st
        return unresolved(jnp.maximum(jnp.maximum(found, zflag),
                                      jnp.where(above < nsel_f, 1.0, 0.0)))

    def tie_body(st):
        lo, hi, thr, found, _, _ = st
        lo, hi, thr, found = bisect_n(4, (lo, hi, thr, found))
        tau = min_at_least(lo)
        return (lo, hi, thr, found, tau, count(lambda s: s > tau))

    _, _, thr, found, tau, above = lax.while_loop(
        tie_cond, tie_body,
        (lo, hi, thr, found, jnp.zeros((1, t), F32), jnp.full((1, t), nsel_f, F32)))

    no_tie = jnp.min(found) == 1.0

    @pl.when(no_tie)
    def _():
        def body(jj, carry):
            sc_ref[span(jj), :] = jnp.where(sc_ref[span(jj), :] >= thr, 0.0, MASK_LOGIT)
            return carry
        lax.fori_loop(0, n_span, body, 0)

    @pl.when(jnp.logical_not(no_tie))
    def _():
        found2 = found > 0.0
        cut = jnp.where(found2, thr, jnp.where(zero_cut, 0.0, tau))
        need = jnp.where(found2, jnp.float32(2 * ATT_BLOCK + 2),
                         nsel_f - jnp.where(zero_cut, n_pos, above))

        def body(jj, seen):
            s = sc_ref[span(jj), :]
            eq = s == cut
            eqf = jnp.where(eq, 1.0, 0.0)
            eqb = eqf.astype(BF16)
            first = jnp.sum(eqf[:t], axis=0, keepdims=True)
            rank = jnp.concatenate(
                [seen + jnp.dot(tri_ref[...], eqb[:t], preferred_element_type=F32),
                 seen + first + jnp.dot(tri_ref[...], eqb[t:], preferred_element_type=F32)],
                axis=0)
            keep = (s > cut) | (eq & (rank < need))
            sc_ref[span(jj), :] = jnp.where(keep, 0.0, MASK_LOGIT)
            total = first + jnp.sum(eqf[t:], axis=0, keepdims=True)
            return seen + jnp.where(found2, 0.0, total)
        lax.fori_loop(0, n_span, body, jnp.zeros((1, t), F32))

    load_masked_heads(q_ref)
    _flash_init(m_ref, l_ref, acc_ref)

    def step(j, carry, near=None):
        for h in range(DSA_HEADS):
            cols = slice(128 * (h // 2), 128 * (h // 2 + 1))
            s_ref[h] = jnp.dot(k_ref[0, _chunk(j), cols], qz_ref[h],
                               preferred_element_type=F32)
        for h in range(DSA_HEADS):
            rows = slice(h * DSA_HEAD_DIM, (h + 1) * DSA_HEAD_DIM)
            s = s_ref[h] + sc_ref[_chunk(j), :]
            if near is not None:
                s = s + bias_ref[near, h]
            _flash_update(m_ref, l_ref, acc_ref, h, s, vt_ref[0, rows, _chunk(j)])
        return carry

    lax.fori_loop(0, i - 1, step, 0)
    lax.fori_loop(jnp.maximum(i - 1, 0), i, functools.partial(step, near=1), 0)
    step(i, 0, near=0)

    for p in range(DSA_HEADS // 2):
        o = jnp.concatenate([acc_ref[h] / l_ref[h] for h in (2 * p, 2 * p + 1)], axis=0)
        out_ref[0, :, 128 * p:128 * (p + 1)] = o.T.astype(out_ref.dtype)


def _dsa(main, iwt, ik2, vt, tab, n_sel):
    b, s, _ = main.shape
    t = ATT_BLOCK
    assert s % (2 * t) == 0, "the score passes walk two key chunks at a time"
    col = lambda c: (lambda bi, i: (bi, i, c))
    full = lambda c: (lambda bi, i: (bi, 0, c))
    return pl.pallas_call(
        functools.partial(_dsa_kernel, n_sel),
        grid=(b, s // t),
        in_specs=[
            pl.BlockSpec(memory_space=pltpu.SMEM),
            pl.BlockSpec((1, t, HEAD_COLS), col(3)),
            pl.BlockSpec((1, IDX_HEADS, t), lambda bi, i: (bi, 0, i)),
            pl.BlockSpec((1, s, 2 * IDX_DIM), full(0)),
            pl.BlockSpec((1, t, HEAD_COLS), col(0)),
            pl.BlockSpec((1, s, HEAD_COLS), full(1)),
            pl.BlockSpec((1, HEAD_COLS, s), lambda bi, i: (bi, 0, 0)),
        ],
        out_specs=pl.BlockSpec((1, t, HEAD_COLS), col(0)),
        out_shape=jax.ShapeDtypeStruct((b, s, HEAD_COLS), BF16),
        scratch_shapes=[
            pltpu.VMEM((s, t), F32),
            pltpu.VMEM((2, DSA_HEADS, t, t), F32),
            pltpu.VMEM((t, t), BF16),
            pltpu.VMEM((DSA_HEADS, 128, t), BF16),
            pltpu.VMEM((DSA_HEADS, 1, t), F32),
            pltpu.VMEM((DSA_HEADS, 1, t), F32),
            pltpu.VMEM((DSA_HEADS, DSA_HEAD_DIM, t), F32),
            pltpu.VMEM((DSA_HEADS, t, t), F32),
        ],
        compiler_params=pltpu.CompilerParams(
            dimension_semantics=("arbitrary", "arbitrary"), vmem_limit_bytes=VMEM_LIMIT),
        name="dsa",
    )(tab, main, iwt, ik2, main, main, vt)


def _diff_kernel(tab_ref, lam_ref, init_ref, subln_ref, q_ref, k_ref, vt_ref, out_ref,
                 bias_ref, qz_ref, m_ref, l_ref, acc_ref, s_ref):
    t = ATT_BLOCK
    i = pl.program_id(1)
    n_maps = 2 * DIFF_HEADS

    @pl.when((pl.program_id(0) == 0) & (i == 0))
    def _():
        _build_rel_bias(tab_ref, bias_ref, DSA_HEADS, DIFF_HEADS)

    key = lax.broadcasted_iota(jnp.int32, (t, t), 0)
    query = lax.broadcasted_iota(jnp.int32, (t, t), 1)
    causal_bias = jnp.where(key <= query, 0.0, MASK_LOGIT)

    q = q_ref[0]
    for m in range(n_maps):
        pair = q[:, 128 * (m // 2):128 * (m // 2 + 1)]
        qz_ref[m] = jnp.where(_half_lane_mask(m % 2), pair, jnp.zeros_like(pair)).T

    lam = lam_ref[...]
    lam_init = init_ref[0:1, 0:1]
    lam_full = (jnp.exp(jnp.sum(lam[0:1] * lam[1:2], axis=-1, keepdims=True))
                - jnp.exp(jnp.sum(lam[2:3] * lam[3:4], axis=-1, keepdims=True)) + lam_init)

    _flash_init(m_ref, l_ref, acc_ref)

    def step(j, carry, near=None):
        for m in range(n_maps):
            cols = slice(128 * (m // 2), 128 * (m // 2 + 1))
            s_ref[m] = jnp.dot(k_ref[0, _chunk(j), cols], qz_ref[m],
                               preferred_element_type=F32)
        for m in range(n_maps):
            cols = slice(128 * (m // 2), 128 * (m // 2 + 1))
            s = s_ref[m]
            if near is not None:
                s = s + bias_ref[near, m // 2]
            if near == 0:
                s = s + causal_bias
            _flash_update(m_ref, l_ref, acc_ref, m, s, vt_ref[0, cols, _chunk(j)])
        return carry

    lax.fori_loop(0, i - 1, step, 0)
    lax.fori_loop(jnp.maximum(i - 1, 0), i, functools.partial(step, near=1), 0)
    step(i, 0, near=0)

    for h in range(DIFF_HEADS):
        cols = slice(128 * h, 128 * (h + 1))
        maps = [acc_ref[2 * h + c] / l_ref[2 * h + c] for c in range(2)]
        o = maps[0] - lam_full * maps[1]
        o = o * lax.rsqrt(jnp.mean(o * o, axis=0, keepdims=True) + EPS) * subln_ref[...]
        o = o * (1.0 - lam_init)
        out_ref[0, :, cols] = o.T.astype(out_ref.dtype)


def _diff(main, vt, tab, lam, init_row, subln_col):
    b, s, _ = main.shape
    t = ATT_BLOCK
    col = lambda c: (lambda bi, i: (bi, i, c))
    full = lambda c: (lambda bi, i: (bi, 0, c))
    const = lambda bi, i: (0, 0)
    return pl.pallas_call(
        _diff_kernel,
        grid=(b, s // t),
        in_specs=[
            pl.BlockSpec(memory_space=pltpu.SMEM),
            pl.BlockSpec((4, DIFF_HEAD_DIM), const),
            pl.BlockSpec((1, 128), const),
            pl.BlockSpec((DIFF_V_DIM, 1), const),
            pl.BlockSpec((1, t, HEAD_COLS), col(4)),
            pl.BlockSpec((1, s, HEAD_COLS), full(5)),
            pl.BlockSpec((1, HEAD_COLS, s), lambda bi, i: (bi, 0, 0)),
        ],
        out_specs=pl.BlockSpec((1, t, HEAD_COLS), col(0)),
        out_shape=jax.ShapeDtypeStruct((b, s, HEAD_COLS), BF16),
        scratch_shapes=[
            pltpu.VMEM((2, DIFF_HEADS, t, t), F32),
            pltpu.VMEM((2 * DIFF_HEADS, 128, t), BF16),
            pltpu.VMEM((2 * DIFF_HEADS, 1, t), F32),
            pltpu.VMEM((2 * DIFF_HEADS, 1, t), F32),
            pltpu.VMEM((2 * DIFF_HEADS, DIFF_V_DIM, t), F32),
            pltpu.VMEM((2 * DIFF_HEADS, t, t), F32),
        ],
        compiler_params=pltpu.CompilerParams(
            dimension_semantics=("arbitrary", "arbitrary"), vmem_limit_bytes=VMEM_LIMIT),
        name="diff",
    )(tab, lam, init_row, subln_col, main, main, vt)


def _ffn_kernel(final, x_ref, a_ref, b_ref, wo_ref, g_ref, wg_ref, wu_ref, wd_ref, gf_ref,
                out_ref, h_ref, y_ref):
    half = wo_ref.shape[0] // 2
    x1 = (x_ref[...]
          + jnp.dot(a_ref[...], wo_ref[:half, :], preferred_element_type=F32)
          + jnp.dot(b_ref[...], wo_ref[half:, :], preferred_element_type=F32))
    h = x1 * lax.rsqrt(jnp.mean(x1 * x1, axis=-1, keepdims=True) + EPS) * g_ref[...]
    h_ref[...] = h.astype(BF16)
    y_ref[...] = x1
    d_ff = wg_ref.shape[1]

    def body(f, carry):
        cols = pl.ds(pl.multiple_of(f * FF_CHUNK, FF_CHUNK), FF_CHUNK)
        hb = h_ref[...]
        g = jnp.dot(hb, wg_ref[:, cols], preferred_element_type=F32)
        u = jnp.dot(hb, wu_ref[:, cols], preferred_element_type=F32)
        act = (g / (1.0 + jnp.exp(-g)) * u).astype(BF16)
        y_ref[...] += jnp.dot(act, wd_ref[cols, :], preferred_element_type=F32)
        return carry

    lax.fori_loop(0, d_ff // FF_CHUNK, body, 0)
    y = y_ref[...]
    if final:
        y = y * lax.rsqrt(jnp.mean(y * y, axis=-1, keepdims=True) + EPS) * gf_ref[...]
    out_ref[...] = y


def _ffn(x2, a, b, wo, g, wg, wu, wd, gf, final):
    t, d = x2.shape
    d_ff = wg.shape[1]
    row = lambda i: (i, 0)
    const = lambda i: (0, 0)
    return pl.pallas_call(
        functools.partial(_ffn_kernel, final),
        grid=(t // ROW_BLOCK,),
        in_specs=[
            pl.BlockSpec((ROW_BLOCK, d), row),
            pl.BlockSpec((ROW_BLOCK, HEAD_COLS), row),
            pl.BlockSpec((ROW_BLOCK, HEAD_COLS), row),
            pl.BlockSpec((2 * HEAD_COLS, d), const),
            pl.BlockSpec((1, d), const),
            pl.BlockSpec((d, d_ff), const),
            pl.BlockSpec((d, d_ff), const),
            pl.BlockSpec((d_ff, d), const),
            pl.BlockSpec((1, d), const),
        ],
        out_specs=pl.BlockSpec((ROW_BLOCK, d), row),
        out_shape=jax.ShapeDtypeStruct((t, d), F32),
        scratch_shapes=[
            pltpu.VMEM((ROW_BLOCK, d), BF16),
            pltpu.VMEM((ROW_BLOCK, d), F32),
        ],
        compiler_params=pltpu.CompilerParams(
            dimension_semantics=("arbitrary",), vmem_limit_bytes=VMEM_LIMIT),
        name="ffn",
    )(x2, a, b, wo, g, wg, wu, wd, gf)


def _pack_w_in(w):
    hd = DSA_HEADS * DSA_HEAD_DIM
    off = 0
    parts = {}
    for name, width in (("dq", hd), ("dk", hd), ("dv", hd), ("iq", IDX_HEADS * IDX_DIM),
                        ("ik", IDX_DIM), ("iw", IDX_HEADS),
                        ("fq", 2 * DIFF_HEADS * DIFF_HEAD_DIM),
                        ("fk", 2 * DIFF_HEADS * DIFF_HEAD_DIM),
                        ("fv", DIFF_HEADS * DIFF_V_DIM)):
        parts[name] = w[:, off:off + width]
        off += width
    pad = jnp.zeros((w.shape[0], N_AUX - IDX_DIM - IDX_HEADS), w.dtype)
    return jnp.concatenate(
        [parts["dq"] * (DSA_HEAD_DIM ** -0.5 * LOG2E), parts["dk"], parts["dv"], parts["iq"],
         parts["fq"] * (DIFF_HEAD_DIM ** -0.5 * LOG2E), parts["fk"], parts["fv"],
         parts["ik"], parts["iw"], pad], axis=1).astype(BF16)


def kernel(x, attn_norm, w_in, diff_lambda, diff_subln, w_out, ffn_norm,
           w_gate, w_up, w_down, rel_bias, final_norm):
    b, s, d = x.shape
    depth = w_in.shape[0]
    n_sel = min(TOPK_MAX, s // 4)
    x2 = x.reshape(b * s, d)
    for l in range(depth):
        lam_init = 0.8 - 0.6 * math.exp(-0.3 * l)
        main, aux = _inproj(x2, attn_norm[l][None, :], _pack_w_in(w_in[l]))
        main = main.reshape(b, s, N_MAIN)
        aux = aux.reshape(b, s, N_AUX)
        ik = aux[:, :, :IDX_DIM].astype(BF16)
        ik2 = jnp.concatenate([ik, ik], axis=-1)
        iwt = jnp.swapaxes(aux[:, :, IDX_DIM:IDX_DIM + IDX_HEADS], 1, 2)
        dvt = jnp.swapaxes(main[:, :, 2 * HEAD_COLS:3 * HEAD_COLS], 1, 2)
        fvt = jnp.swapaxes(main[:, :, 6 * HEAD_COLS:7 * HEAD_COLS], 1, 2)
        dsa_out = _dsa(main, iwt, ik2, dvt, rel_bias, n_sel)
        diff_out = _diff(main, fvt, rel_bias, diff_lambda[l],
                         jnp.full((1, 128), lam_init, F32), diff_subln[l][:, None])
        x2 = _ffn(x2, dsa_out.reshape(b * s, HEAD_COLS), diff_out.reshape(b * s, HEAD_COLS),
                  w_out[l].astype(BF16), ffn_norm[l][None, :], w_gate[l].astype(BF16),
                  w_up[l].astype(BF16), w_down[l].astype(BF16), final_norm[None, :],
                  final=(l == depth - 1))
    return x2.reshape(b, s, d)
```

```python
import functools
import math

import jax
import jax.numpy as jnp
from jax import lax
from jax.experimental import pallas as pl
from jax.experimental.pallas import tpu as pltpu

DSA_HEADS = 8
DSA_HEAD_DIM = 64
IDX_HEADS = 8
IDX_DIM = 64
TOPK_MAX = 256
DIFF_HEADS = 4
DIFF_HEAD_DIM = 64
DIFF_V_DIM = 2 * DIFF_HEAD_DIM
NUM_BUCKETS = 32
MAX_DISTANCE = 128
EPS = 1e-6

HEAD_COLS = 512
N_MAIN = 7 * HEAD_COLS
N_AUX = 128
ATT_BLOCK = 256
ROW_BLOCK = 512
FF_CHUNK = 256

MASK_SCORE = -3.0e38
SELECT_ALL = -1.0e38
MASK_LOGIT = -1.0e30
LOG2E = math.log2(math.e)
ATT_GROUPS = 2
SEARCH_FIRST_STEPS = 16
SEARCH_EXTRA_ROUNDS = 12
VMEM_LIMIT = 56 * 1024 * 1024

F32 = jnp.float32
BF16 = jnp.bfloat16

def _inproj_kernel(x_ref, g_ref, w_ref, main_ref, aux_ref):
    x = x_ref[...]
    h = x * lax.rsqrt(jnp.mean(x * x, axis=-1, keepdims=True) + EPS) * g_ref[...]
    h = h.astype(BF16)
    for c in range(0, N_MAIN, HEAD_COLS):
        main_ref[:, c:c + HEAD_COLS] = jnp.dot(
            h, w_ref[:, c:c + HEAD_COLS], preferred_element_type=F32).astype(BF16)
    aux_ref[...] = jnp.dot(h, w_ref[:, N_MAIN:], preferred_element_type=F32)


def _inproj(x2, g, w_all):
    t, d = x2.shape
    return pl.pallas_call(
        _inproj_kernel,
        grid=(t // ROW_BLOCK,),
        in_specs=[
            pl.BlockSpec((ROW_BLOCK, d), lambda i: (i, 0)),
            pl.BlockSpec((1, d), lambda i: (0, 0)),
            pl.BlockSpec((d, N_MAIN + N_AUX), lambda i: (0, 0)),
        ],
        out_specs=[
            pl.BlockSpec((ROW_BLOCK, N_MAIN), lambda i: (i, 0)),
            pl.BlockSpec((ROW_BLOCK, N_AUX), lambda i: (i, 0)),
        ],
        out_shape=[
            jax.ShapeDtypeStruct((t, N_MAIN), BF16),
            jax.ShapeDtypeStruct((t, N_AUX), F32),
        ],
        compiler_params=pltpu.CompilerParams(
            dimension_semantics=("arbitrary",), vmem_limit_bytes=VMEM_LIMIT),
        name="inproj",
    )(x2, g, w_all)


def _build_rel_bias(tab_ref, bias_near_ref, head0, n_heads):
    t = ATT_BLOCK
    key = lax.broadcasted_iota(jnp.int32, (t, t), 0)
    query = lax.broadcasted_iota(jnp.int32, (t, t), 1)
    max_exact = NUM_BUCKETS // 2
    for d in range(2):
        n = jnp.maximum(query - key + d * t, 0)
        nf = jnp.maximum(n, 1).astype(F32)
        large = max_exact + (jnp.log(nf / max_exact) / math.log(MAX_DISTANCE / max_exact)
                             * (NUM_BUCKETS - max_exact)).astype(jnp.int32)
        large = jnp.minimum(large, NUM_BUCKETS - 1)
        bucket = jnp.where(n < max_exact, n, large)
        for h in range(n_heads):
            bias_near_ref[d, h] = jnp.zeros((t, t), F32)

        def fill(b, carry, d=d, bucket=bucket):
            hit = bucket == b
            for h in range(n_heads):
                val = (tab_ref[b, head0 + h] - tab_ref[NUM_BUCKETS - 1, head0 + h]) * LOG2E
                bias_near_ref[d, h] = jnp.where(hit, val, bias_near_ref[d, h])
            return carry

        lax.fori_loop(0, NUM_BUCKETS - 1, fill, 0)


def _chunk(j):
    return pl.ds(pl.multiple_of(j * ATT_BLOCK, ATT_BLOCK), ATT_BLOCK)


def _half_lane_mask(half):
    lane = lax.broadcasted_iota(jnp.int32, (ATT_BLOCK, 128), 1)
    return (lane >= 64 * half) & (lane < 64 * (half + 1))


def _flash_update(m_ref, l_ref, acc_ref, idx, s, vt):
    m = m_ref[idx]
    m_new = jnp.maximum(m, jnp.max(s, axis=0, keepdims=True))
    alpha = jnp.exp2(m - m_new)
    p = jnp.exp2(s - m_new)
    l_ref[idx] = alpha * l_ref[idx] + jnp.sum(p, axis=0, keepdims=True)
    acc_ref[idx] = alpha * acc_ref[idx] + jnp.dot(vt, p.astype(BF16), preferred_element_type=F32)
    m_ref[idx] = m_new


def _flash_init(m_ref, l_ref, acc_ref):
    m_ref[...] = jnp.full(m_ref.shape, MASK_LOGIT, F32)
    l_ref[...] = jnp.zeros(l_ref.shape, F32)
    acc_ref[...] = jnp.zeros(acc_ref.shape, F32)


def _dsa_kernel(n_sel, tab_ref, iq_ref, iwt_ref, ik_ref, q_ref, k_ref, vt_ref, out_ref,
                sc_ref, bias_ref, tri_ref, qz_ref, m_ref, l_ref, acc_ref, s_ref):
    t = ATT_BLOCK
    i = pl.program_id(1)

    @pl.when((pl.program_id(0) == 0) & (i == 0))
    def _():
        _build_rel_bias(tab_ref, bias_ref, 0, DSA_HEADS)
        r = lax.broadcasted_iota(jnp.int32, (t, t), 0)
        c = lax.broadcasted_iota(jnp.int32, (t, t), 1)
        tri_ref[...] = jnp.where(c < r, 1.0, 0.0).astype(BF16)

    key = lax.broadcasted_iota(jnp.int32, (t, t), 0)
    query = lax.broadcasted_iota(jnp.int32, (t, t), 1)
    causal = key <= query
    nsel_f = jnp.float32(n_sel)
    big = jnp.float32(3.0e38)

    def load_masked_heads(src_ref):
        x = src_ref[0]
        for h in range(DSA_HEADS):
            pair = x[:, 128 * (h // 2):128 * (h // 2 + 1)]
            qz_ref[h] = jnp.where(_half_lane_mask(h % 2), pair, jnp.zeros_like(pair)).T

    load_masked_heads(iq_ref)
    w = iwt_ref[0] * (IDX_HEADS ** -0.5 * IDX_DIM ** -0.5)

    idx_groups = [range(g * (IDX_HEADS // ATT_GROUPS), (g + 1) * (IDX_HEADS // ATT_GROUPS))
                  for g in range(ATT_GROUPS)]

    def index_dots(j, heads):
        kc = ik_ref[0, _chunk(j), :]
        for h in heads:
            s_ref[h] = jnp.dot(kc, qz_ref[h], preferred_element_type=F32)

    def score_tile(j, last=False):
        acc = jnp.zeros((t, t), F32)
        for g in range(ATT_GROUPS):
            if g + 1 < ATT_GROUPS:
                index_dots(j, idx_groups[g + 1])
            elif not last:
                index_dots(j + 1, idx_groups[0])
            for h in idx_groups[g]:
                acc = acc + w[h:h + 1, :] * jnp.maximum(s_ref[h], 0.0)
        return acc

    def score_body(j, carry):
        lo, hi = carry
        acc = score_tile(j)
        sc_ref[_chunk(j), :] = acc
        return (jnp.minimum(lo, jnp.min(acc, axis=0, keepdims=True)),
                jnp.maximum(hi, jnp.max(acc, axis=0, keepdims=True)))

    index_dots(0, idx_groups[0])
    lo, hi = lax.fori_loop(0, i, score_body,
                           (jnp.full((1, t), big, F32), jnp.full((1, t), -big, F32)))
    acc = score_tile(i, last=True)
    sc_ref[_chunk(i), :] = jnp.where(causal, acc, MASK_SCORE)
    lo = jnp.minimum(lo, jnp.min(jnp.where(causal, acc, big), axis=0, keepdims=True))
    hi = jnp.maximum(hi, jnp.max(jnp.where(causal, acc, -big), axis=0, keepdims=True))
    hi = hi + jnp.maximum(jnp.abs(hi), 1.0e-30)

    @pl.when(i % 2 == 0)
    def _():
        sc_ref[_chunk(i + 1), :] = jnp.full((t, t), MASK_SCORE, F32)

    n_span = i // 2 + 1

    def span(jj):
        return pl.ds(pl.multiple_of(jj * (2 * t), 2 * t), 2 * t)

    def fold(x):
        return x.reshape(x.shape[0] // 8, 8, t)

    def count(pred_fn):
        def body(jj, acc):
            for r in range(0, 2 * t, t // 2):
                rows = pl.ds(pl.multiple_of(jj * (2 * t) + r, t // 2), t // 2)
                ind = jnp.where(pred_fn(sc_ref[rows, :]), 1.0, 0.0)
                acc = acc + jnp.sum(fold(ind), axis=0)
            return acc
        part = lax.fori_loop(0, n_span, body, jnp.zeros((8, t), F32))
        return jnp.sum(part, axis=0, keepdims=True)

    def min_at_least(thr):
        def body(jj, acc):
            s = sc_ref[span(jj), :]
            return jnp.minimum(acc, jnp.min(fold(jnp.where(s >= thr, s, big)), axis=0))
        part = lax.fori_loop(0, n_span, body, jnp.full((8, t), big, F32))
        return jnp.min(part, axis=0, keepdims=True)

    pos = i * t + lax.broadcasted_iota(jnp.int32, (1, t), 1)
    small = pos < n_sel

    def count_pos_zero():
        def body(jj, acc):
            s = sc_ref[span(jj), :]
            return (acc[0] + jnp.sum(fold(jnp.where(s > 0.0, 1.0, 0.0)), axis=0),
                    acc[1] + jnp.sum(fold(jnp.where(s == 0.0, 1.0, 0.0)), axis=0))
        zero = jnp.zeros((8, t), F32)
        a, b = lax.fori_loop(0, n_span, body, (zero, zero))
        return jnp.sum(a, axis=0, keepdims=True), jnp.sum(b, axis=0, keepdims=True)

    n_pos, n_zero = count_pos_zero()
    zero_cut = (n_pos < nsel_f) & (n_pos + n_zero >= nsel_f) & jnp.logical_not(small)
    zflag = jnp.where(zero_cut, 1.0, 0.0)
    lo = jnp.where(n_pos >= nsel_f, 0.0, lo)
    hi = jnp.where(n_pos + n_zero < nsel_f, 0.0, hi)

    def bisect(st):
        lo, hi, thr, found = st
        mid = 0.5 * lo + 0.5 * hi
        cnt = count(lambda s: s >= mid)
        up = cnt >= nsel_f
        hit = cnt == nsel_f
        thr = jnp.where(hit & (found == 0.0), mid, thr)
        found = jnp.where(hit, 1.0, found)
        return (jnp.where(up, mid, lo), jnp.where(up, hi, mid), thr, found)

    def bisect_n(n, st):
        return lax.fori_loop(0, n, lambda _, s: bisect(s), st)

    def unresolved(flag):
        return jnp.min(flag) == 0.0

    def search_cond(st):
        it, _, _, _, found = st
        return (it < SEARCH_EXTRA_ROUNDS) & unresolved(jnp.maximum(found, zflag))

    def search_body(st):
        it, lo, hi, thr, found = st
        lo, hi, thr, found = bisect_n(2, (lo, hi, thr, found))
        return (it + 1, lo, hi, thr, found)

    thr0 = jnp.full((1, t), SELECT_ALL, F32)
    st = bisect_n(SEARCH_FIRST_STEPS, (lo, hi, thr0, jnp.where(small, 1.0, 0.0)))
    _, lo, hi, thr, found = lax.while_loop(search_cond, search_body, (jnp.int32(0),) + st)

    def tie_cond(st):
        _, _, _, found, _, above = st
        return unresolved(jnp.maximum(jnp.maximum(found, zflag),
                                      jnp.where(above < nsel_f, 1.0, 0.0)))

    def tie_body(st):
        lo, hi, thr, found, _, _ = st
        lo, hi, thr, found = bisect_n(4, (lo, hi, thr, found))
        tau = min_at_least(lo)
        return (lo, hi, thr, found, tau, count(lambda s: s > tau))

    _, _, thr, found, tau, above = lax.while_loop(
        tie_cond, tie_body,
        (lo, hi, thr, found, jnp.zeros((1, t), F32), jnp.full((1, t), nsel_f, F32)))

    no_tie = jnp.min(found) == 1.0

    @pl.when(no_tie)
    def _():
        def body(jj, carry):
            sc_ref[span(jj), :] = jnp.where(sc_ref[span(jj), :] >= thr, 0.0, MASK_LOGIT)
            return carry
        lax.fori_loop(0, n_span, body, 0)

    @pl.when(jnp.logical_not(no_tie))
    def _():
        found2 = found > 0.0
        cut = jnp.where(found2, thr, jnp.where(zero_cut, 0.0, tau))
        need = jnp.where(found2, jnp.float32(2 * ATT_BLOCK + 2),
                         nsel_f - jnp.where(zero_cut, n_pos, above))

        def body(jj, seen):
            s = sc_ref[span(jj), :]
            eq = s == cut
            eqf = jnp.where(eq, 1.0, 0.0)
            eqb = eqf.astype(BF16)
            first = jnp.sum(eqf[:t], axis=0, keepdims=True)
            rank = jnp.concatenate(
                [seen + jnp.dot(tri_ref[...], eqb[:t], preferred_element_type=F32),
                 seen + first + jnp.dot(tri_ref[...], eqb[t:], preferred_element_type=F32)],
                axis=0)
            keep = (s > cut) | (eq & (rank < need))
            sc_ref[span(jj), :] = jnp.where(keep, 0.0, MASK_LOGIT)
            total = first + jnp.sum(eqf[t:], axis=0, keepdims=True)
            return seen + jnp.where(found2, 0.0, total)
        lax.fori_loop(0, n_span, body, jnp.zeros((1, t), F32))

    load_masked_heads(q_ref)
    _flash_init(m_ref, l_ref, acc_ref)

    per_group = DSA_HEADS // ATT_GROUPS
    groups = [range(g * per_group, (g + 1) * per_group) for g in range(ATT_GROUPS)]

    def logits(j, heads):
        for h in heads:
            cols = slice(128 * (h // 2), 128 * (h // 2 + 1))
            s_ref[h] = jnp.dot(k_ref[0, _chunk(j), cols], qz_ref[h],
                               preferred_element_type=F32)

    def softmax_pv(j, heads, near):
        for h in heads:
            rows = slice(h * DSA_HEAD_DIM, (h + 1) * DSA_HEAD_DIM)
            s = s_ref[h] + sc_ref[_chunk(j), :]
            if near is not None:
                s = s + bias_ref[near, h]
            _flash_update(m_ref, l_ref, acc_ref, h, s, vt_ref[0, rows, _chunk(j)])

    def step(j, carry, near=None, last=False):
        for g in range(ATT_GROUPS):
            if g + 1 < ATT_GROUPS:
                logits(j, groups[g + 1])
            elif not last:
                logits(j + 1, groups[0])
            softmax_pv(j, groups[g], near)
        return carry

    logits(0, groups[0])
    lax.fori_loop(0, i - 1, step, 0)
    lax.fori_loop(jnp.maximum(i - 1, 0), i, functools.partial(step, near=1), 0)
    step(i, 0, near=0, last=True)

    for p in range(DSA_HEADS // 2):
        o = jnp.concatenate([acc_ref[h] / l_ref[h] for h in (2 * p, 2 * p + 1)], axis=0)
        out_ref[0, :, 128 * p:128 * (p + 1)] = o.T.astype(out_ref.dtype)


def _dsa(main, iwt, ik2, vt, tab, n_sel):
    b, s, _ = main.shape
    t = ATT_BLOCK
    assert s % (2 * t) == 0, "the score passes walk two key chunks at a time"
    col = lambda c: (lambda bi, i: (bi, i, c))
    full = lambda c: (lambda bi, i: (bi, 0, c))
    return pl.pallas_call(
        functools.partial(_dsa_kernel, n_sel),
        grid=(b, s // t),
        in_specs=[
            pl.BlockSpec(memory_space=pltpu.SMEM),
            pl.BlockSpec((1, t, HEAD_COLS), col(3)),
            pl.BlockSpec((1, IDX_HEADS, t), lambda bi, i: (bi, 0, i)),
            pl.BlockSpec((1, s, 2 * IDX_DIM), full(0)),
            pl.BlockSpec((1, t, HEAD_COLS), col(0)),
            pl.BlockSpec((1, s, HEAD_COLS), full(1)),
            pl.BlockSpec((1, HEAD_COLS, s), lambda bi, i: (bi, 0, 0)),
        ],
        out_specs=pl.BlockSpec((1, t, HEAD_COLS), col(0)),
        out_shape=jax.ShapeDtypeStruct((b, s, HEAD_COLS), BF16),
        scratch_shapes=[
            pltpu.VMEM((s, t), F32),
            pltpu.VMEM((2, DSA_HEADS, t, t), F32),
            pltpu.VMEM((t, t), BF16),
            pltpu.VMEM((DSA_HEADS, 128, t), BF16),
            pltpu.VMEM((DSA_HEADS, 1, t), F32),
            pltpu.VMEM((DSA_HEADS, 1, t), F32),
            pltpu.VMEM((DSA_HEADS, DSA_HEAD_DIM, t), F32),
            pltpu.VMEM((DSA_HEADS, t, t), F32),
        ],
        compiler_params=pltpu.CompilerParams(
            dimension_semantics=("arbitrary", "arbitrary"), vmem_limit_bytes=VMEM_LIMIT),
        name="dsa",
    )(tab, main, iwt, ik2, main, main, vt)


def _diff_kernel(tab_ref, lam_ref, init_ref, subln_ref, q_ref, k_ref, vt_ref, out_ref,
                 bias_ref, qz_ref, m_ref, l_ref, acc_ref, s_ref):
    t = ATT_BLOCK
    i = pl.program_id(1)
    n_maps = 2 * DIFF_HEADS

    @pl.when((pl.program_id(0) == 0) & (i == 0))
    def _():
        _build_rel_bias(tab_ref, bias_ref, DSA_HEADS, DIFF_HEADS)

    key = lax.broadcasted_iota(jnp.int32, (t, t), 0)
    query = lax.broadcasted_iota(jnp.int32, (t, t), 1)
    causal_bias = jnp.where(key <= query, 0.0, MASK_LOGIT)

    q = q_ref[0]
    for m in range(n_maps):
        pair = q[:, 128 * (m // 2):128 * (m // 2 + 1)]
        qz_ref[m] = jnp.where(_half_lane_mask(m % 2), pair, jnp.zeros_like(pair)).T

    lam = lam_ref[...]
    lam_init = init_ref[0:1, 0:1]
    lam_full = (jnp.exp(jnp.sum(lam[0:1] * lam[1:2], axis=-1, keepdims=True))
                - jnp.exp(jnp.sum(lam[2:3] * lam[3:4], axis=-1, keepdims=True)) + lam_init)

    _flash_init(m_ref, l_ref, acc_ref)

    per_group = n_maps // ATT_GROUPS
    groups = [range(g * per_group, (g + 1) * per_group) for g in range(ATT_GROUPS)]

    def logits(j, maps):
        for m in maps:
            cols = slice(128 * (m // 2), 128 * (m // 2 + 1))
            s_ref[m] = jnp.dot(k_ref[0, _chunk(j), cols], qz_ref[m],
                               preferred_element_type=F32)

    def softmax_pv(j, maps, near):
        for m in maps:
            cols = slice(128 * (m // 2), 128 * (m // 2 + 1))
            s = s_ref[m]
            if near is not None:
                s = s + bias_ref[near, m // 2]
            if near == 0:
                s = s + causal_bias
            _flash_update(m_ref, l_ref, acc_ref, m, s, vt_ref[0, cols, _chunk(j)])

    def step(j, carry, near=None, last=False):
        for g in range(ATT_GROUPS):
            if g + 1 < ATT_GROUPS:
                logits(j, groups[g + 1])
            elif not last:
                logits(j + 1, groups[0])
            softmax_pv(j, groups[g], near)
        return carry

    logits(0, groups[0])
    lax.fori_loop(0, i - 1, step, 0)
    lax.fori_loop(jnp.maximum(i - 1, 0), i, functools.partial(step, near=1), 0)
    step(i, 0, near=0, last=True)

    for h in range(DIFF_HEADS):
        cols = slice(128 * h, 128 * (h + 1))
        maps = [acc_ref[2 * h + c] / l_ref[2 * h + c] for c in range(2)]
        o = maps[0] - lam_full * maps[1]
        o = o * lax.rsqrt(jnp.mean(o * o, axis=0, keepdims=True) + EPS) * subln_ref[...]
        o = o * (1.0 - lam_init)
        out_ref[0, :, cols] = o.T.astype(out_ref.dtype)


def _diff(main, vt, tab, lam, init_row, subln_col):
    b, s, _ = main.shape
    t = ATT_BLOCK
    col = lambda c: (lambda bi, i: (bi, i, c))
    full = lambda c: (lambda bi, i: (bi, 0, c))
    const = lambda bi, i: (0, 0)
    return pl.pallas_call(
        _diff_kernel,
        grid=(b, s // t),
        in_specs=[
            pl.BlockSpec(memory_space=pltpu.SMEM),
            pl.BlockSpec((4, DIFF_HEAD_DIM), const),
            pl.BlockSpec((1, 128), const),
            pl.BlockSpec((DIFF_V_DIM, 1), const),
            pl.BlockSpec((1, t, HEAD_COLS), col(4)),
            pl.BlockSpec((1, s, HEAD_COLS), full(5)),
            pl.BlockSpec((1, HEAD_COLS, s), lambda bi, i: (bi, 0, 0)),
        ],
        out_specs=pl.BlockSpec((1, t, HEAD_COLS), col(0)),
        out_shape=jax.ShapeDtypeStruct((b, s, HEAD_COLS), BF16),
        scratch_shapes=[
            pltpu.VMEM((2, DIFF_HEADS, t, t), F32),
            pltpu.VMEM((2 * DIFF_HEADS, 128, t), BF16),
            pltpu.VMEM((2 * DIFF_HEADS, 1, t), F32),
            pltpu.VMEM((2 * DIFF_HEADS, 1, t), F32),
            pltpu.VMEM((2 * DIFF_HEADS, DIFF_V_DIM, t), F32),
            pltpu.VMEM((2 * DIFF_HEADS, t, t), F32),
        ],
        compiler_params=pltpu.CompilerParams(
            dimension_semantics=("arbitrary", "arbitrary"), vmem_limit_bytes=VMEM_LIMIT),
        name="diff",
    )(tab, lam, init_row, subln_col, main, main, vt)


def _ffn_kernel(final, x_ref, a_ref, b_ref, wo_ref, g_ref, wg_ref, wu_ref, wd_ref, gf_ref,
                out_ref, h_ref, y_ref):
    half = wo_ref.shape[0] // 2
    x1 = (x_ref[...]
          + jnp.dot(a_ref[...], wo_ref[:half, :], preferred_element_type=F32)
          + jnp.dot(b_ref[...], wo_ref[half:, :], preferred_element_type=F32))
    h = x1 * lax.rsqrt(jnp.mean(x1 * x1, axis=-1, keepdims=True) + EPS) * g_ref[...]
    h_ref[...] = h.astype(BF16)
    y_ref[...] = x1
    d_ff = wg_ref.shape[1]

    def body(f, carry):
        cols = pl.ds(pl.multiple_of(f * FF_CHUNK, FF_CHUNK), FF_CHUNK)
        hb = h_ref[...]
        g = jnp.dot(hb, wg_ref[:, cols], preferred_element_type=F32)
        u = jnp.dot(hb, wu_ref[:, cols], preferred_element_type=F32)
        act = (g / (1.0 + jnp.exp(-g)) * u).astype(BF16)
        y_ref[...] += jnp.dot(act, wd_ref[cols, :], preferred_element_type=F32)
        return carry

    lax.fori_loop(0, d_ff // FF_CHUNK, body, 0)
    y = y_ref[...]
    if final:
        y = y * lax.rsqrt(jnp.mean(y * y, axis=-1, keepdims=True) + EPS) * gf_ref[...]
    out_ref[...] = y


def _ffn(x2, a, b, wo, g, wg, wu, wd, gf, final):
    t, d = x2.shape
    d_ff = wg.shape[1]
    row = lambda i: (i, 0)
    const = lambda i: (0, 0)
    return pl.pallas_call(
        functools.partial(_ffn_kernel, final),
        grid=(t // ROW_BLOCK,),
        in_specs=[
            pl.BlockSpec((ROW_BLOCK, d), row),
            pl.BlockSpec((ROW_BLOCK, HEAD_COLS), row),
            pl.BlockSpec((ROW_BLOCK, HEAD_COLS), row),
            pl.BlockSpec((2 * HEAD_COLS, d), const),
            pl.BlockSpec((1, d), const),
            pl.BlockSpec((d, d_ff), const),
            pl.BlockSpec((d, d_ff), const),
            pl.BlockSpec((d_ff, d), const),
            pl.BlockSpec((1, d), const),
        ],
        out_specs=pl.BlockSpec((ROW_BLOCK, d), row),
        out_shape=jax.ShapeDtypeStruct((t, d), F32),
        scratch_shapes=[
            pltpu.VMEM((ROW_BLOCK, d), BF16),
            pltpu.VMEM((ROW_BLOCK, d), F32),
        ],
        compiler_params=pltpu.CompilerParams(
            dimension_semantics=("arbitrary",), vmem_limit_bytes=VMEM_LIMIT),
        name="ffn",
    )(x2, a, b, wo, g, wg, wu, wd, gf)


def _pack_w_in(w):
    hd = DSA_HEADS * DSA_HEAD_DIM
    off = 0
    parts = {}
    for name, width in (("dq", hd), ("dk", hd), ("dv", hd), ("iq", IDX_HEADS * IDX_DIM),
                        ("ik", IDX_DIM), ("iw", IDX_HEADS),
                        ("fq", 2 * DIFF_HEADS * DIFF_HEAD_DIM),
                        ("fk", 2 * DIFF_HEADS * DIFF_HEAD_DIM),
                        ("fv", DIFF_HEADS * DIFF_V_DIM)):
        parts[name] = w[:, off:off + width]
        off += width
    pad = jnp.zeros((w.shape[0], N_AUX - IDX_DIM - IDX_HEADS), w.dtype)
    return jnp.concatenate(
        [parts["dq"] * (DSA_HEAD_DIM ** -0.5 * LOG2E), parts["dk"], parts["dv"], parts["iq"],
         parts["fq"] * (DIFF_HEAD_DIM ** -0.5 * LOG2E), parts["fk"], parts["fv"],
         parts["ik"], parts["iw"], pad], axis=1).astype(BF16)


def kernel(x, attn_norm, w_in, diff_lambda, diff_subln, w_out, ffn_norm,
           w_gate, w_up, w_down, rel_bias, final_norm):
    b, s, d = x.shape
    depth = w_in.shape[0]
    n_sel = min(TOPK_MAX, s // 4)
    x2 = x.reshape(b * s, d)
    for l in range(depth):
        lam_init = 0.8 - 0.6 * math.exp(-0.3 * l)
        main, aux = _inproj(x2, attn_norm[l][None, :], _pack_w_in(w_in[l]))
        main = main.reshape(b, s, N_MAIN)
        aux = aux.reshape(b, s, N_AUX)
        ik = aux[:, :, :IDX_DIM].astype(BF16)
        ik2 = jnp.concatenate([ik, ik], axis=-1)
        iwt = jnp.swapaxes(aux[:, :, IDX_DIM:IDX_DIM + IDX_HEADS], 1, 2)
        dvt = jnp.swapaxes(main[:, :, 2 * HEAD_COLS:3 * HEAD_COLS], 1, 2)
        fvt = jnp.swapaxes(main[:, :, 6 * HEAD_COLS:7 * HEAD_COLS], 1, 2)
        dsa_out = _dsa(main, iwt, ik2, dvt, rel_bias, n_sel)
        diff_out = _diff(main, fvt, rel_bias, diff_lambda[l],
                         jnp.full((1, 128), lam_init, F32), diff_subln[l][:, None])
        x2 = _ffn(x2, dsa_out.reshape(b * s, HEAD_COLS), diff_out.reshape(b * s, HEAD_COLS),
                  w_out[l].astype(BF16), ffn_norm[l][None, :], w_gate[l].astype(BF16),
                  w_up[l].astype(BF16), w_down[l].astype(BF16), final_norm[None, :],
                  final=(l == depth - 1))
    return x2.reshape(b, s, d)
```

```python
import functools
import math

import jax
import jax.numpy as jnp
from jax import lax
from jax.experimental import pallas as pl
from jax.experimental.pallas import tpu as pltpu

DSA_HEADS = 8
DSA_HEAD_DIM = 64
IDX_HEADS = 8
IDX_DIM = 64
TOPK_MAX = 256
DIFF_HEADS = 4
DIFF_HEAD_DIM = 64
DIFF_V_DIM = 2 * DIFF_HEAD_DIM
NUM_BUCKETS = 32
MAX_DISTANCE = 128
EPS = 1e-6

HEAD_COLS = 512
N_MAIN = 7 * HEAD_COLS
N_AUX = 128
ATT_BLOCK = 256
ROW_BLOCK = 512
FF_CHUNK = 256

MASK_SCORE = -3.0e38
SELECT_ALL = -1.0e38
MASK_LOGIT = -1.0e30
LOG2E = math.log2(math.e)
ATT_GROUPS = 2
SEARCH_FIRST_STEPS = 16
SEARCH_EXTRA_ROUNDS = 12
VMEM_LIMIT = 56 * 1024 * 1024

F32 = jnp.float32
BF16 = jnp.bfloat16

def _inproj_kernel(x_ref, g_ref, w_ref, main_ref, aux_ref):
    x = x_ref[...]
    h = x * lax.rsqrt(jnp.mean(x * x, axis=-1, keepdims=True) + EPS) * g_ref[...]
    h = h.astype(BF16)
    for c in range(0, N_MAIN, HEAD_COLS):
        main_ref[:, c:c + HEAD_COLS] = jnp.dot(
            h, w_ref[:, c:c + HEAD_COLS], preferred_element_type=F32).astype(BF16)
    aux_ref[...] = jnp.dot(h, w_ref[:, N_MAIN:], preferred_element_type=F32)


def _resident(shape, index_map, single_buffer):
    if single_buffer:
        return pl.BlockSpec(shape, index_map, pipeline_mode=pl.Buffered(1))
    return pl.BlockSpec(shape, index_map)


def _inproj(x2, g, w_all, row_block=ROW_BLOCK, single_buffer=False):
    t, d = x2.shape
    return pl.pallas_call(
        _inproj_kernel,
        grid=(t // row_block,),
        in_specs=[
            pl.BlockSpec((row_block, d), lambda i: (i, 0)),
            pl.BlockSpec((1, d), lambda i: (0, 0)),
            _resident((d, N_MAIN + N_AUX), lambda i: (0, 0), single_buffer),
        ],
        out_specs=[
            pl.BlockSpec((row_block, N_MAIN), lambda i: (i, 0)),
            pl.BlockSpec((row_block, N_AUX), lambda i: (i, 0)),
        ],
        out_shape=[
            jax.ShapeDtypeStruct((t, N_MAIN), BF16),
            jax.ShapeDtypeStruct((t, N_AUX), F32),
        ],
        compiler_params=pltpu.CompilerParams(
            dimension_semantics=("arbitrary",), vmem_limit_bytes=VMEM_LIMIT),
        name="inproj",
    )(x2, g, w_all)


def _build_rel_bias(tab_ref, bias_near_ref, head0, n_heads):
    t = ATT_BLOCK
    key = lax.broadcasted_iota(jnp.int32, (t, t), 0)
    query = lax.broadcasted_iota(jnp.int32, (t, t), 1)
    max_exact = NUM_BUCKETS // 2
    for d in range(2):
        n = jnp.maximum(query - key + d * t, 0)
        nf = jnp.maximum(n, 1).astype(F32)
        large = max_exact + (jnp.log(nf / max_exact) / math.log(MAX_DISTANCE / max_exact)
                             * (NUM_BUCKETS - max_exact)).astype(jnp.int32)
        large = jnp.minimum(large, NUM_BUCKETS - 1)
        bucket = jnp.where(n < max_exact, n, large)
        for h in range(n_heads):
            bias_near_ref[d, h] = jnp.zeros((t, t), F32)

        def fill(b, carry, d=d, bucket=bucket):
            hit = bucket == b
            for h in range(n_heads):
                val = (tab_ref[b, head0 + h] - tab_ref[NUM_BUCKETS - 1, head0 + h]) * LOG2E
                bias_near_ref[d, h] = jnp.where(hit, val, bias_near_ref[d, h])
            return carry

        lax.fori_loop(0, NUM_BUCKETS - 1, fill, 0)


def _chunk(j):
    return pl.ds(pl.multiple_of(j * ATT_BLOCK, ATT_BLOCK), ATT_BLOCK)


def _half_lane_mask(half):
    lane = lax.broadcasted_iota(jnp.int32, (ATT_BLOCK, 128), 1)
    return (lane >= 64 * half) & (lane < 64 * (half + 1))


def _flash_update(m_ref, l_ref, acc_ref, idx, s, vt):
    m = m_ref[idx]
    m_new = jnp.maximum(m, jnp.max(s, axis=0, keepdims=True))
    alpha = jnp.exp2(m - m_new)
    p = jnp.exp2(s - m_new)
    l_ref[idx] = alpha * l_ref[idx] + jnp.sum(p, axis=0, keepdims=True)
    acc_ref[idx] = alpha * acc_ref[idx] + jnp.dot(vt, p.astype(BF16), preferred_element_type=F32)
    m_ref[idx] = m_new


def _flash_init(m_ref, l_ref, acc_ref):
    m_ref[...] = jnp.full(m_ref.shape, MASK_LOGIT, F32)
    l_ref[...] = jnp.zeros(l_ref.shape, F32)
    acc_ref[...] = jnp.zeros(acc_ref.shape, F32)


def _dsa_kernel(n_sel, score_groups, att_groups,
                tab_ref, iq_ref, iwt_ref, ik_ref, q_ref, k_ref, vt_ref, out_ref,
                sc_ref, bias_ref, tri_ref, qz_ref, m_ref, l_ref, acc_ref, s_ref):
    t = ATT_BLOCK
    i = pl.program_id(1)

    @pl.when((pl.program_id(0) == 0) & (i == 0))
    def _():
        _build_rel_bias(tab_ref, bias_ref, 0, DSA_HEADS)
        r = lax.broadcasted_iota(jnp.int32, (t, t), 0)
        c = lax.broadcasted_iota(jnp.int32, (t, t), 1)
        tri_ref[...] = jnp.where(c < r, 1.0, 0.0).astype(BF16)

    key = lax.broadcasted_iota(jnp.int32, (t, t), 0)
    query = lax.broadcasted_iota(jnp.int32, (t, t), 1)
    causal = key <= query
    nsel_f = jnp.float32(n_sel)
    big = jnp.float32(3.0e38)

    def load_masked_heads(src_ref):
        x = src_ref[0]
        for h in range(DSA_HEADS):
            pair = x[:, 128 * (h // 2):128 * (h // 2 + 1)]
            qz_ref[h] = jnp.where(_half_lane_mask(h % 2), pair, jnp.zeros_like(pair)).T

    load_masked_heads(iq_ref)
    w = iwt_ref[0] * (IDX_HEADS ** -0.5 * IDX_DIM ** -0.5)

    idx_groups = [range(g * (IDX_HEADS // score_groups), (g + 1) * (IDX_HEADS // score_groups))
                  for g in range(score_groups)]

    def index_dots(j, heads):
        kc = ik_ref[0, _chunk(j), :]
        for h in heads:
            s_ref[h] = jnp.dot(kc, qz_ref[h], preferred_element_type=F32)

    def score_tile(j, last=False):
        acc = jnp.zeros((t, t), F32)
        if score_groups == 1:
            kc = ik_ref[0, _chunk(j), :]
            for h in range(IDX_HEADS):
                d = jnp.dot(kc, qz_ref[h], preferred_element_type=F32)
                acc = acc + w[h:h + 1, :] * jnp.maximum(d, 0.0)
            return acc
        for g in range(score_groups):
            if g + 1 < score_groups:
                index_dots(j, idx_groups[g + 1])
            elif not last:
                index_dots(j + 1, idx_groups[0])
            for h in idx_groups[g]:
                acc = acc + w[h:h + 1, :] * jnp.maximum(s_ref[h], 0.0)
        return acc

    def score_body(j, carry):
        lo, hi = carry
        acc = score_tile(j)
        sc_ref[_chunk(j), :] = acc
        return (jnp.minimum(lo, jnp.min(acc, axis=0, keepdims=True)),
                jnp.maximum(hi, jnp.max(acc, axis=0, keepdims=True)))

    if score_groups > 1:
        index_dots(0, idx_groups[0])
    lo, hi = lax.fori_loop(0, i, score_body,
                           (jnp.full((1, t), big, F32), jnp.full((1, t), -big, F32)))
    acc = score_tile(i, last=True)
    sc_ref[_chunk(i), :] = jnp.where(causal, acc, MASK_SCORE)
    lo = jnp.minimum(lo, jnp.min(jnp.where(causal, acc, big), axis=0, keepdims=True))
    hi = jnp.maximum(hi, jnp.max(jnp.where(causal, acc, -big), axis=0, keepdims=True))
    hi = hi + jnp.maximum(jnp.abs(hi), 1.0e-30)

    @pl.when(i % 2 == 0)
    def _():
        sc_ref[_chunk(i + 1), :] = jnp.full((t, t), MASK_SCORE, F32)

    n_span = i // 2 + 1

    def span(jj):
        return pl.ds(pl.multiple_of(jj * (2 * t), 2 * t), 2 * t)

    def fold(x):
        return x.reshape(x.shape[0] // 8, 8, t)

    def count(pred_fn):
        def body(jj, acc):
            for r in range(0, 2 * t, t // 2):
                rows = pl.ds(pl.multiple_of(jj * (2 * t) + r, t // 2), t // 2)
                ind = jnp.where(pred_fn(sc_ref[rows, :]), 1.0, 0.0)
                acc = acc + jnp.sum(fold(ind), axis=0)
            return acc
        part = lax.fori_loop(0, n_span, body, jnp.zeros((8, t), F32))
        return jnp.sum(part, axis=0, keepdims=True)

    def min_at_least(thr):
        def body(jj, acc):
            s = sc_ref[span(jj), :]
            return jnp.minimum(acc, jnp.min(fold(jnp.where(s >= thr, s, big)), axis=0))
        part = lax.fori_loop(0, n_span, body, jnp.full((8, t), big, F32))
        return jnp.min(part, axis=0, keepdims=True)

    pos = i * t + lax.broadcasted_iota(jnp.int32, (1, t), 1)
    small = pos < n_sel

    def count_pos_zero():
        def body(jj, acc):
            s = sc_ref[span(jj), :]
            return (acc[0] + jnp.sum(fold(jnp.where(s > 0.0, 1.0, 0.0)), axis=0),
                    acc[1] + jnp.sum(fold(jnp.where(s == 0.0, 1.0, 0.0)), axis=0))
        zero = jnp.zeros((8, t), F32)
        a, b = lax.fori_loop(0, n_span, body, (zero, zero))
        return jnp.sum(a, axis=0, keepdims=True), jnp.sum(b, axis=0, keepdims=True)

    n_pos, n_zero = count_pos_zero()
    zero_cut = (n_pos < nsel_f) & (n_pos + n_zero >= nsel_f) & jnp.logical_not(small)
    zflag = jnp.where(zero_cut, 1.0, 0.0)
    lo = jnp.where(n_pos >= nsel_f, 0.0, lo)
    hi = jnp.where(n_pos + n_zero < nsel_f, 0.0, hi)

    def bisect(st):
        lo, hi, thr, found = st
        mid = 0.5 * lo + 0.5 * hi
        cnt = count(lambda s: s >= mid)
        up = cnt >= nsel_f
        hit = cnt == nsel_f
        thr = jnp.where(hit & (found == 0.0), mid, thr)
        found = jnp.where(hit, 1.0, found)
        return (jnp.where(up, mid, lo), jnp.where(up, hi, mid), thr, found)

    def bisect_n(n, st):
        return lax.fori_loop(0, n, lambda _, s: bisect(s), st)

    def unresolved(flag):
        return jnp.min(flag) == 0.0

    def search_cond(st):
        it, _, _, _, found = st
        return (it < SEARCH_EXTRA_ROUNDS) & unresolved(jnp.maximum(found, zflag))

    def search_body(st):
        it, lo, hi, thr, found = st
        lo, hi, thr, found = bisect_n(2, (lo, hi, thr, found))
        return (it + 1, lo, hi, thr, found)

    thr0 = jnp.full((1, t), SELECT_ALL, F32)
    st = bisect_n(SEARCH_FIRST_STEPS, (lo, hi, thr0, jnp.where(small, 1.0, 0.0)))
    _, lo, hi, thr, found = lax.while_loop(search_cond, search_body, (jnp.int32(0),) + st)

    def tie_cond(st):
        _, _, _, found, _, above = st
        return unresolved(jnp.maximum(jnp.maximum(found, zflag),
                                      jnp.where(above < nsel_f, 1.0, 0.0)))

    def tie_body(st):
        lo, hi, thr, found, _, _ = st
        lo, hi, thr, found = bisect_n(4, (lo, hi, thr, found))
        tau = min_at_least(lo)
        return (lo, hi, thr, found, tau, count(lambda s: s > tau))

    _, _, thr, found, tau, above = lax.while_loop(
        tie_cond, tie_body,
        (lo, hi, thr, found, jnp.zeros((1, t), F32), jnp.full((1, t), nsel_f, F32)))

    no_tie = jnp.min(found) == 1.0

    @pl.when(no_tie)
    def _():
        def body(jj, carry):
            sc_ref[span(jj), :] = jnp.where(sc_ref[span(jj), :] >= thr, 0.0, MASK_LOGIT)
            return carry
        lax.fori_loop(0, n_span, body, 0)

    @pl.when(jnp.logical_not(no_tie))
    def _():
        found2 = found > 0.0
        cut = jnp.where(found2, thr, jnp.where(zero_cut, 0.0, tau))
        need = jnp.where(found2, jnp.float32(2 * ATT_BLOCK + 2),
                         nsel_f - jnp.where(zero_cut, n_pos, above))

        def body(jj, seen):
            s = sc_ref[span(jj), :]
            eq = s == cut
            eqf = jnp.where(eq, 1.0, 0.0)
            eqb = eqf.astype(BF16)
            first = jnp.sum(eqf[:t], axis=0, keepdims=True)
            rank = jnp.concatenate(
                [seen + jnp.dot(tri_ref[...], eqb[:t], preferred_element_type=F32),
                 seen + first + jnp.dot(tri_ref[...], eqb[t:], preferred_element_type=F32)],
                axis=0)
            keep = (s > cut) | (eq & (rank < need))
            sc_ref[span(jj), :] = jnp.where(keep, 0.0, MASK_LOGIT)
            total = first + jnp.sum(eqf[t:], axis=0, keepdims=True)
            return seen + jnp.where(found2, 0.0, total)
        lax.fori_loop(0, n_span, body, jnp.zeros((1, t), F32))

    load_masked_heads(q_ref)
    _flash_init(m_ref, l_ref, acc_ref)

    per_group = DSA_HEADS // att_groups
    groups = [range(g * per_group, (g + 1) * per_group) for g in range(att_groups)]

    def logits(j, heads):
        for h in heads:
            cols = slice(128 * (h // 2), 128 * (h // 2 + 1))
            s_ref[h] = jnp.dot(k_ref[0, _chunk(j), cols], qz_ref[h],
                               preferred_element_type=F32)

    def softmax_pv(j, heads, near):
        for h in heads:
            rows = slice(h * DSA_HEAD_DIM, (h + 1) * DSA_HEAD_DIM)
            s = s_ref[h] + sc_ref[_chunk(j), :]
            if near is not None:
                s = s + bias_ref[near, h]
            _flash_update(m_ref, l_ref, acc_ref, h, s, vt_ref[0, rows, _chunk(j)])

    def step(j, carry, near=None, last=False):
        if att_groups == 1:
            logits(j, groups[0])
            softmax_pv(j, groups[0], near)
            return carry
        for g in range(att_groups):
            if g + 1 < att_groups:
                logits(j, groups[g + 1])
            elif not last:
                logits(j + 1, groups[0])
            softmax_pv(j, groups[g], near)
        return carry

    if att_groups > 1:
        logits(0, groups[0])
    lax.fori_loop(0, i - 1, step, 0)
    lax.fori_loop(jnp.maximum(i - 1, 0), i, functools.partial(step, near=1), 0)
    step(i, 0, near=0, last=True)

    for p in range(DSA_HEADS // 2):
        o = jnp.concatenate([acc_ref[h] / l_ref[h] for h in (2 * p, 2 * p + 1)], axis=0)
        out_ref[0, :, 128 * p:128 * (p + 1)] = o.T.astype(out_ref.dtype)


def _dsa(main, iwt, ik2, vt, tab, n_sel, score_groups=ATT_GROUPS, att_groups=ATT_GROUPS):
    b, s, _ = main.shape
    t = ATT_BLOCK
    assert s % (2 * t) == 0, "the score passes walk two key chunks at a time"
    col = lambda c: (lambda bi, i: (bi, i, c))
    full = lambda c: (lambda bi, i: (bi, 0, c))
    return pl.pallas_call(
        functools.partial(_dsa_kernel, n_sel, score_groups, att_groups),
        grid=(b, s // t),
        in_specs=[
            pl.BlockSpec(memory_space=pltpu.SMEM),
            pl.BlockSpec((1, t, HEAD_COLS), col(3)),
            pl.BlockSpec((1, IDX_HEADS, t), lambda bi, i: (bi, 0, i)),
            pl.BlockSpec((1, s, 2 * IDX_DIM), full(0)),
            pl.BlockSpec((1, t, HEAD_COLS), col(0)),
            pl.BlockSpec((1, s, HEAD_COLS), full(1)),
            pl.BlockSpec((1, HEAD_COLS, s), lambda bi, i: (bi, 0, 0)),
        ],
        out_specs=pl.BlockSpec((1, t, HEAD_COLS), col(0)),
        out_shape=jax.ShapeDtypeStruct((b, s, HEAD_COLS), BF16),
        scratch_shapes=[
            pltpu.VMEM((s, t), F32),
            pltpu.VMEM((2, DSA_HEADS, t, t), F32),
            pltpu.VMEM((t, t), BF16),
            pltpu.VMEM((DSA_HEADS, 128, t), BF16),
            pltpu.VMEM((DSA_HEADS, 1, t), F32),
            pltpu.VMEM((DSA_HEADS, 1, t), F32),
            pltpu.VMEM((DSA_HEADS, DSA_HEAD_DIM, t), F32),
            pltpu.VMEM((DSA_HEADS, t, t), F32),
        ],
        compiler_params=pltpu.CompilerParams(
            dimension_semantics=("arbitrary", "arbitrary"), vmem_limit_bytes=VMEM_LIMIT),
        name="dsa",
    )(tab, main, iwt, ik2, main, main, vt)


def _diff_kernel(tab_ref, lam_ref, init_ref, subln_ref, q_ref, k_ref, vt_ref, out_ref,
                 bias_ref, qz_ref, m_ref, l_ref, acc_ref, s_ref):
    t = ATT_BLOCK
    i = pl.program_id(1)
    n_maps = 2 * DIFF_HEADS

    @pl.when((pl.program_id(0) == 0) & (i == 0))
    def _():
        _build_rel_bias(tab_ref, bias_ref, DSA_HEADS, DIFF_HEADS)

    key = lax.broadcasted_iota(jnp.int32, (t, t), 0)
    query = lax.broadcasted_iota(jnp.int32, (t, t), 1)
    causal_bias = jnp.where(key <= query, 0.0, MASK_LOGIT)

    q = q_ref[0]
    for m in range(n_maps):
        pair = q[:, 128 * (m // 2):128 * (m // 2 + 1)]
        qz_ref[m] = jnp.where(_half_lane_mask(m % 2), pair, jnp.zeros_like(pair)).T

    lam = lam_ref[...]
    lam_init = init_ref[0:1, 0:1]
    lam_full = (jnp.exp(jnp.sum(lam[0:1] * lam[1:2], axis=-1, keepdims=True))
                - jnp.exp(jnp.sum(lam[2:3] * lam[3:4], axis=-1, keepdims=True)) + lam_init)

    _flash_init(m_ref, l_ref, acc_ref)

    per_group = n_maps // ATT_GROUPS
    groups = [range(g * per_group, (g + 1) * per_group) for g in range(ATT_GROUPS)]

    def logits(j, maps):
        for m in maps:
            cols = slice(128 * (m // 2), 128 * (m // 2 + 1))
            s_ref[m] = jnp.dot(k_ref[0, _chunk(j), cols], qz_ref[m],
                               preferred_element_type=F32)

    def softmax_pv(j, maps, near):
        for m in maps:
            cols = slice(128 * (m // 2), 128 * (m // 2 + 1))
            s = s_ref[m]
            if near is not None:
                s = s + bias_ref[near, m // 2]
            if near == 0:
                s = s + causal_bias
            _flash_update(m_ref, l_ref, acc_ref, m, s, vt_ref[0, cols, _chunk(j)])

    def step(j, carry, near=None, last=False):
        for g in range(ATT_GROUPS):
            if g + 1 < ATT_GROUPS:
                logits(j, groups[g + 1])
            elif not last:
                logits(j + 1, groups[0])
            softmax_pv(j, groups[g], near)
        return carry

    logits(0, groups[0])
    lax.fori_loop(0, i - 1, step, 0)
    lax.fori_loop(jnp.maximum(i - 1, 0), i, functools.partial(step, near=1), 0)
    step(i, 0, near=0, last=True)

    for h in range(DIFF_HEADS):
        cols = slice(128 * h, 128 * (h + 1))
        maps = [acc_ref[2 * h + c] / l_ref[2 * h + c] for c in range(2)]
        o = maps[0] - lam_full * maps[1]
        o = o * lax.rsqrt(jnp.mean(o * o, axis=0, keepdims=True) + EPS) * subln_ref[...]
        o = o * (1.0 - lam_init)
        out_ref[0, :, cols] = o.T.astype(out_ref.dtype)


def _diff(main, vt, tab, lam, init_row, subln_col):
    b, s, _ = main.shape
    t = ATT_BLOCK
    col = lambda c: (lambda bi, i: (bi, i, c))
    full = lambda c: (lambda bi, i: (bi, 0, c))
    const = lambda bi, i: (0, 0)
    return pl.pallas_call(
        _diff_kernel,
        grid=(b, s // t),
        in_specs=[
            pl.BlockSpec(memory_space=pltpu.SMEM),
            pl.BlockSpec((4, DIFF_HEAD_DIM), const),
            pl.BlockSpec((1, 128), const),
            pl.BlockSpec((DIFF_V_DIM, 1), const),
            pl.BlockSpec((1, t, HEAD_COLS), col(4)),
            pl.BlockSpec((1, s, HEAD_COLS), full(5)),
            pl.BlockSpec((1, HEAD_COLS, s), lambda bi, i: (bi, 0, 0)),
        ],
        out_specs=pl.BlockSpec((1, t, HEAD_COLS), col(0)),
        out_shape=jax.ShapeDtypeStruct((b, s, HEAD_COLS), BF16),
        scratch_shapes=[
            pltpu.VMEM((2, DIFF_HEADS, t, t), F32),
            pltpu.VMEM((2 * DIFF_HEADS, 128, t), BF16),
            pltpu.VMEM((2 * DIFF_HEADS, 1, t), F32),
            pltpu.VMEM((2 * DIFF_HEADS, 1, t), F32),
            pltpu.VMEM((2 * DIFF_HEADS, DIFF_V_DIM, t), F32),
            pltpu.VMEM((2 * DIFF_HEADS, t, t), F32),
        ],
        compiler_params=pltpu.CompilerParams(
            dimension_semantics=("arbitrary", "arbitrary"), vmem_limit_bytes=VMEM_LIMIT),
        name="diff",
    )(tab, lam, init_row, subln_col, main, main, vt)


def _ffn_kernel(final, ff_chunk,
                x_ref, a_ref, b_ref, wo_ref, g_ref, wg_ref, wu_ref, wd_ref, gf_ref,
                out_ref, h_ref, y_ref):
    half = wo_ref.shape[0] // 2
    x1 = (x_ref[...]
          + jnp.dot(a_ref[...], wo_ref[:half, :], preferred_element_type=F32)
          + jnp.dot(b_ref[...], wo_ref[half:, :], preferred_element_type=F32))
    h = x1 * lax.rsqrt(jnp.mean(x1 * x1, axis=-1, keepdims=True) + EPS) * g_ref[...]
    h_ref[...] = h.astype(BF16)
    y_ref[...] = x1
    d_ff = wg_ref.shape[1]

    def body(f, carry):
        cols = pl.ds(pl.multiple_of(f * ff_chunk, ff_chunk), ff_chunk)
        hb = h_ref[...]
        g = jnp.dot(hb, wg_ref[:, cols], preferred_element_type=F32)
        u = jnp.dot(hb, wu_ref[:, cols], preferred_element_type=F32)
        act = (g / (1.0 + jnp.exp(-g)) * u).astype(BF16)
        y_ref[...] += jnp.dot(act, wd_ref[cols, :], preferred_element_type=F32)
        return carry

    lax.fori_loop(0, d_ff // ff_chunk, body, 0)
    y = y_ref[...]
    if final:
        y = y * lax.rsqrt(jnp.mean(y * y, axis=-1, keepdims=True) + EPS) * gf_ref[...]
    out_ref[...] = y


def _ffn(x2, a, b, wo, g, wg, wu, wd, gf, final, row_block=ROW_BLOCK, ff_chunk=FF_CHUNK,
         single_buffer=False):
    t, d = x2.shape
    d_ff = wg.shape[1]
    assert d_ff % ff_chunk == 0
    row = lambda i: (i, 0)
    const = lambda i: (0, 0)
    return pl.pallas_call(
        functools.partial(_ffn_kernel, final, ff_chunk),
        grid=(t // row_block,),
        in_specs=[
            pl.BlockSpec((row_block, d), row),
            pl.BlockSpec((row_block, HEAD_COLS), row),
            pl.BlockSpec((row_block, HEAD_COLS), row),
            _resident((2 * HEAD_COLS, d), const, single_buffer),
            pl.BlockSpec((1, d), const),
            _resident((d, d_ff), const, single_buffer),
            _resident((d, d_ff), const, single_buffer),
            _resident((d_ff, d), const, single_buffer),
            pl.BlockSpec((1, d), const),
        ],
        out_specs=pl.BlockSpec((row_block, d), row),
        out_shape=jax.ShapeDtypeStruct((t, d), F32),
        scratch_shapes=[
            pltpu.VMEM((row_block, d), BF16),
            pltpu.VMEM((row_block, d), F32),
        ],
        compiler_params=pltpu.CompilerParams(
            dimension_semantics=("arbitrary",), vmem_limit_bytes=VMEM_LIMIT),
        name="ffn",
    )(x2, a, b, wo, g, wg, wu, wd, gf)


def _pack_w_in(w):
    hd = DSA_HEADS * DSA_HEAD_DIM
    off = 0
    parts = {}
    for name, width in (("dq", hd), ("dk", hd), ("dv", hd), ("iq", IDX_HEADS * IDX_DIM),
                        ("ik", IDX_DIM), ("iw", IDX_HEADS),
                        ("fq", 2 * DIFF_HEADS * DIFF_HEAD_DIM),
                        ("fk", 2 * DIFF_HEADS * DIFF_HEAD_DIM),
                        ("fv", DIFF_HEADS * DIFF_V_DIM)):
        parts[name] = w[:, off:off + width]
        off += width
    pad = jnp.zeros((w.shape[0], N_AUX - IDX_DIM - IDX_HEADS), w.dtype)
    return jnp.concatenate(
        [parts["dq"] * (DSA_HEAD_DIM ** -0.5 * LOG2E), parts["dk"], parts["dv"], parts["iq"],
         parts["fq"] * (DIFF_HEAD_DIM ** -0.5 * LOG2E), parts["fk"], parts["fv"],
         parts["ik"], parts["iw"], pad], axis=1).astype(BF16)


AB_VARIANTS = (
    dict(dsa=dict(score_groups=1, att_groups=1), ffn=dict(row_block=512, ff_chunk=256),
         inproj=dict(row_block=512)),
    dict(dsa=dict(score_groups=1, att_groups=2), ffn=dict(row_block=512, ff_chunk=1408),
         inproj=dict(row_block=1024)),
    dict(dsa=dict(score_groups=2, att_groups=1), ffn=dict(row_block=256, ff_chunk=256),
         inproj=dict(row_block=256)),
    dict(dsa=dict(score_groups=2, att_groups=2),
         ffn=dict(row_block=512, ff_chunk=256, single_buffer=True),
         inproj=dict(row_block=512, single_buffer=True)),
)


def kernel(x, attn_norm, w_in, diff_lambda, diff_subln, w_out, ffn_norm,
           w_gate, w_up, w_down, rel_bias, final_norm):
    b, s, d = x.shape
    depth = w_in.shape[0]
    n_sel = min(TOPK_MAX, s // 4)
    x2 = x.reshape(b * s, d)
    for l in range(depth):
        lam_init = 0.8 - 0.6 * math.exp(-0.3 * l)
        ab = AB_VARIANTS[l % len(AB_VARIANTS)]
        main, aux = _inproj(x2, attn_norm[l][None, :], _pack_w_in(w_in[l]), **ab["inproj"])
        main = main.reshape(b, s, N_MAIN)
        aux = aux.reshape(b, s, N_AUX)
        ik = aux[:, :, :IDX_DIM].astype(BF16)
        ik2 = jnp.concatenate([ik, ik], axis=-1)
        iwt = jnp.swapaxes(aux[:, :, IDX_DIM:IDX_DIM + IDX_HEADS], 1, 2)
        dvt = jnp.swapaxes(main[:, :, 2 * HEAD_COLS:3 * HEAD_COLS], 1, 2)
        fvt = jnp.swapaxes(main[:, :, 6 * HEAD_COLS:7 * HEAD_COLS], 1, 2)
        dsa_out = _dsa(main, iwt, ik2, dvt, rel_bias, n_sel, **ab["dsa"])
        diff_out = _diff(main, fvt, rel_bias, diff_lambda[l],
                         jnp.full((1, 128), lam_init, F32), diff_subln[l][:, None])
        x2 = _ffn(x2, dsa_out.reshape(b * s, HEAD_COLS), diff_out.reshape(b * s, HEAD_COLS),
                  w_out[l].astype(BF16), ffn_norm[l][None, :], w_gate[l].astype(BF16),
                  w_up[l].astype(BF16), w_down[l].astype(BF16), final_norm[None, :],
                  final=(l == depth - 1), **ab["ffn"])
    return x2.reshape(b, s, d)
```

```python
import functools
import math

import jax
import jax.numpy as jnp
from jax import lax
from jax.experimental import pallas as pl
from jax.experimental.pallas import tpu as pltpu

DSA_HEADS = 8
DSA_HEAD_DIM = 64
IDX_HEADS = 8
IDX_DIM = 64
TOPK_MAX = 256
DIFF_HEADS = 4
DIFF_HEAD_DIM = 64
DIFF_V_DIM = 2 * DIFF_HEAD_DIM
NUM_BUCKETS = 32
MAX_DISTANCE = 128
EPS = 1e-6

HEAD_COLS = 512
N_MAIN = 7 * HEAD_COLS
N_AUX = 128
ATT_BLOCK = 256
ROW_BLOCK = 512
IN_ROW_BLOCK = 1024
FF_CHUNK = 256

MASK_SCORE = -3.0e38
SELECT_ALL = -1.0e38
MASK_LOGIT = -1.0e30
LOG2E = math.log2(math.e)
SUM_ROWS = 16
ATT_GROUPS = 2
SEARCH_FIRST_STEPS = 16
SEARCH_EXTRA_ROUNDS = 12
VMEM_LIMIT = 56 * 1024 * 1024

F32 = jnp.float32
BF16 = jnp.bfloat16

def _inproj_kernel(x_ref, g_ref, w_ref, main_ref, aux_ref):
    x = x_ref[...]
    h = x * lax.rsqrt(jnp.mean(x * x, axis=-1, keepdims=True) + EPS) * g_ref[...]
    h = h.astype(BF16)
    for c in range(0, N_MAIN, HEAD_COLS):
        main_ref[:, c:c + HEAD_COLS] = jnp.dot(
            h, w_ref[:, c:c + HEAD_COLS], preferred_element_type=F32).astype(BF16)
    aux_ref[...] = jnp.dot(h, w_ref[:, N_MAIN:], preferred_element_type=F32)


def _resident(shape, index_map, single_buffer):
    if single_buffer:
        return pl.BlockSpec(shape, index_map, pipeline_mode=pl.Buffered(1))
    return pl.BlockSpec(shape, index_map)


def _inproj(x2, g, w_all, row_block=IN_ROW_BLOCK, single_buffer=False):
    t, d = x2.shape
    return pl.pallas_call(
        _inproj_kernel,
        grid=(t // row_block,),
        in_specs=[
            pl.BlockSpec((row_block, d), lambda i: (i, 0)),
            pl.BlockSpec((1, d), lambda i: (0, 0)),
            _resident((d, N_MAIN + N_AUX), lambda i: (0, 0), single_buffer),
        ],
        out_specs=[
            pl.BlockSpec((row_block, N_MAIN), lambda i: (i, 0)),
            pl.BlockSpec((row_block, N_AUX), lambda i: (i, 0)),
        ],
        out_shape=[
            jax.ShapeDtypeStruct((t, N_MAIN), BF16),
            jax.ShapeDtypeStruct((t, N_AUX), F32),
        ],
        compiler_params=pltpu.CompilerParams(
            dimension_semantics=("arbitrary",), vmem_limit_bytes=VMEM_LIMIT),
        name="inproj",
    )(x2, g, w_all)


def _build_rel_bias(tab_ref, bias_near_ref, head0, n_heads):
    t = ATT_BLOCK
    key = lax.broadcasted_iota(jnp.int32, (t, t), 0)
    query = lax.broadcasted_iota(jnp.int32, (t, t), 1)
    max_exact = NUM_BUCKETS // 2
    for d in range(2):
        n = jnp.maximum(query - key + d * t, 0)
        nf = jnp.maximum(n, 1).astype(F32)
        large = max_exact + (jnp.log(nf / max_exact) / math.log(MAX_DISTANCE / max_exact)
                             * (NUM_BUCKETS - max_exact)).astype(jnp.int32)
        large = jnp.minimum(large, NUM_BUCKETS - 1)
        bucket = jnp.where(n < max_exact, n, large)
        for h in range(n_heads):
            bias_near_ref[d, h] = jnp.zeros((t, t), F32)

        def fill(b, carry, d=d, bucket=bucket):
            hit = bucket == b
            for h in range(n_heads):
                val = (tab_ref[b, head0 + h] - tab_ref[NUM_BUCKETS - 1, head0 + h]) * LOG2E
                bias_near_ref[d, h] = jnp.where(hit, val, bias_near_ref[d, h])
            return carry

        lax.fori_loop(0, NUM_BUCKETS - 1, fill, 0)


def _chunk(j):
    return pl.ds(pl.multiple_of(j * ATT_BLOCK, ATT_BLOCK), ATT_BLOCK)


def _half_lane_mask(half):
    lane = lax.broadcasted_iota(jnp.int32, (ATT_BLOCK, 128), 1)
    return (lane >= 64 * half) & (lane < 64 * (half + 1))


def _flash_update(m_ref, l_ref, acc_ref, idx, s, vt, mxu_sum=False):
    m = m_ref[idx]
    m_new = jnp.maximum(m, jnp.max(s, axis=0, keepdims=True))
    alpha = jnp.exp2(m - m_new)
    p = jnp.exp2(s - m_new)
    if not mxu_sum:
        l_ref[idx] = alpha * l_ref[idx] + jnp.sum(p, axis=0, keepdims=True)
    acc_ref[idx] = alpha * acc_ref[idx] + jnp.dot(vt, p.astype(BF16), preferred_element_type=F32)
    m_ref[idx] = m_new


def _normalised(l_ref, acc_ref, idx, dv, mxu_sum):
    acc = acc_ref[idx]
    if mxu_sum:
        return acc[:dv] / acc[dv:dv + 1]
    return acc / l_ref[idx]


def _augment_vt(vt, heads):
    b, hd, s = vt.shape
    dv = hd // heads
    v4 = vt.reshape(b, heads, dv, s)
    ones = jnp.ones((b, heads, 1, s), vt.dtype)
    zeros = jnp.zeros((b, heads, SUM_ROWS - 1, s), vt.dtype)
    return jnp.concatenate([v4, ones, zeros], axis=2).reshape(b, heads * (dv + SUM_ROWS), s)


def _flash_init(m_ref, l_ref, acc_ref):
    m_ref[...] = jnp.full(m_ref.shape, MASK_LOGIT, F32)
    l_ref[...] = jnp.zeros(l_ref.shape, F32)
    acc_ref[...] = jnp.zeros(acc_ref.shape, F32)


def _dsa_kernel(n_sel, score_groups, att_groups, first_steps, mxu_sum,
                tab_ref, iq_ref, iwt_ref, ik_ref, q_ref, k_ref, vt_ref, out_ref,
                sc_ref, bias_ref, tri_ref, qz_ref, m_ref, l_ref, acc_ref, s_ref):
    t = ATT_BLOCK
    i = pl.program_id(1)

    @pl.when((pl.program_id(0) == 0) & (i == 0))
    def _():
        _build_rel_bias(tab_ref, bias_ref, 0, DSA_HEADS)
        r = lax.broadcasted_iota(jnp.int32, (t, t), 0)
        c = lax.broadcasted_iota(jnp.int32, (t, t), 1)
        tri_ref[...] = jnp.where(c < r, 1.0, 0.0).astype(BF16)

    key = lax.broadcasted_iota(jnp.int32, (t, t), 0)
    query = lax.broadcasted_iota(jnp.int32, (t, t), 1)
    causal = key <= query
    nsel_f = jnp.float32(n_sel)
    big = jnp.float32(3.0e38)

    def load_masked_heads(src_ref):
        x = src_ref[0]
        for h in range(DSA_HEADS):
            pair = x[:, 128 * (h // 2):128 * (h // 2 + 1)]
            qz_ref[h] = jnp.where(_half_lane_mask(h % 2), pair, jnp.zeros_like(pair)).T

    load_masked_heads(iq_ref)
    w = iwt_ref[0] * (IDX_HEADS ** -0.5 * IDX_DIM ** -0.5)

    idx_groups = [range(g * (IDX_HEADS // score_groups), (g + 1) * (IDX_HEADS // score_groups))
                  for g in range(score_groups)]

    def index_dots(j, heads):
        kc = ik_ref[0, _chunk(j), :]
        for h in heads:
            s_ref[h] = jnp.dot(kc, qz_ref[h], preferred_element_type=F32)

    def score_tile(j, last=False):
        acc = jnp.zeros((t, t), F32)
        if score_groups == 1:
            kc = ik_ref[0, _chunk(j), :]
            for h in range(IDX_HEADS):
                d = jnp.dot(kc, qz_ref[h], preferred_element_type=F32)
                acc = acc + w[h:h + 1, :] * jnp.maximum(d, 0.0)
            return acc
        for g in range(score_groups):
            if g + 1 < score_groups:
                index_dots(j, idx_groups[g + 1])
            elif not last:
                index_dots(j + 1, idx_groups[0])
            for h in idx_groups[g]:
                acc = acc + w[h:h + 1, :] * jnp.maximum(s_ref[h], 0.0)
        return acc

    def score_body(j, carry):
        lo, hi = carry
        acc = score_tile(j)
        sc_ref[_chunk(j), :] = acc
        return (jnp.minimum(lo, jnp.min(acc, axis=0, keepdims=True)),
                jnp.maximum(hi, jnp.max(acc, axis=0, keepdims=True)))

    if score_groups > 1:
        index_dots(0, idx_groups[0])
    lo, hi = lax.fori_loop(0, i, score_body,
                           (jnp.full((1, t), big, F32), jnp.full((1, t), -big, F32)))
    acc = score_tile(i, last=True)
    sc_ref[_chunk(i), :] = jnp.where(causal, acc, MASK_SCORE)
    lo = jnp.minimum(lo, jnp.min(jnp.where(causal, acc, big), axis=0, keepdims=True))
    hi = jnp.maximum(hi, jnp.max(jnp.where(causal, acc, -big), axis=0, keepdims=True))
    hi = hi + jnp.maximum(jnp.abs(hi), 1.0e-30)

    @pl.when(i % 2 == 0)
    def _():
        sc_ref[_chunk(i + 1), :] = jnp.full((t, t), MASK_SCORE, F32)

    n_span = i // 2 + 1

    def span(jj):
        return pl.ds(pl.multiple_of(jj * (2 * t), 2 * t), 2 * t)

    def fold(x):
        return x.reshape(x.shape[0] // 8, 8, t)

    def count(pred_fn):
        def body(jj, acc):
            for r in range(0, 2 * t, t // 2):
                rows = pl.ds(pl.multiple_of(jj * (2 * t) + r, t // 2), t // 2)
                ind = jnp.where(pred_fn(sc_ref[rows, :]), 1.0, 0.0)
                acc = acc + jnp.sum(fold(ind), axis=0)
            return acc
        part = lax.fori_loop(0, n_span, body, jnp.zeros((8, t), F32))
        return jnp.sum(part, axis=0, keepdims=True)

    def min_at_least(thr):
        def body(jj, acc):
            s = sc_ref[span(jj), :]
            return jnp.minimum(acc, jnp.min(fold(jnp.where(s >= thr, s, big)), axis=0))
        part = lax.fori_loop(0, n_span, body, jnp.full((8, t), big, F32))
        return jnp.min(part, axis=0, keepdims=True)

    pos = i * t + lax.broadcasted_iota(jnp.int32, (1, t), 1)
    small = pos < n_sel

    def count_pos_zero():
        def body(jj, acc):
            s = sc_ref[span(jj), :]
            return (acc[0] + jnp.sum(fold(jnp.where(s > 0.0, 1.0, 0.0)), axis=0),
                    acc[1] + jnp.sum(fold(jnp.where(s == 0.0, 1.0, 0.0)), axis=0))
        zero = jnp.zeros((8, t), F32)
        a, b = lax.fori_loop(0, n_span, body, (zero, zero))
        return jnp.sum(a, axis=0, keepdims=True), jnp.sum(b, axis=0, keepdims=True)

    n_pos, n_zero = count_pos_zero()
    zero_cut = (n_pos < nsel_f) & (n_pos + n_zero >= nsel_f) & jnp.logical_not(small)
    zflag = jnp.where(zero_cut, 1.0, 0.0)
    lo = jnp.where(n_pos >= nsel_f, 0.0, lo)
    hi = jnp.where(n_pos + n_zero < nsel_f, 0.0, hi)

    def bisect(st):
        lo, hi, thr, found = st
        mid = 0.5 * lo + 0.5 * hi
        cnt = count(lambda s: s >= mid)
        up = cnt >= nsel_f
        hit = cnt == nsel_f
        thr = jnp.where(hit & (found == 0.0), mid, thr)
        found = jnp.where(hit, 1.0, found)
        return (jnp.where(up, mid, lo), jnp.where(up, hi, mid), thr, found)

    def bisect_n(n, st):
        return lax.fori_loop(0, n, lambda _, s: bisect(s), st)

    def unresolved(flag):
        return jnp.min(flag) == 0.0

    def search_cond(st):
        it, _, _, _, found = st
        return (it < SEARCH_EXTRA_ROUNDS) & unresolved(jnp.maximum(found, zflag))

    def search_body(st):
        it, lo, hi, thr, found = st
        lo, hi, thr, found = bisect_n(2, (lo, hi, thr, found))
        return (it + 1, lo, hi, thr, found)

    thr0 = jnp.full((1, t), SELECT_ALL, F32)
    st = bisect_n(first_steps, (lo, hi, thr0, jnp.where(small, 1.0, 0.0)))
    _, lo, hi, thr, found = lax.while_loop(search_cond, search_body, (jnp.int32(0),) + st)

    def tie_cond(st):
        _, _, _, found, _, above = st
        return unresolved(jnp.maximum(jnp.maximum(found, zflag),
                                      jnp.where(above < nsel_f, 1.0, 0.0)))

    def tie_body(st):
        lo, hi, thr, found, _, _ = st
        lo, hi, thr, found = bisect_n(4, (lo, hi, thr, found))
        tau = min_at_least(lo)
        return (lo, hi, thr, found, tau, count(lambda s: s > tau))

    _, _, thr, found, tau, above = lax.while_loop(
        tie_cond, tie_body,
        (lo, hi, thr, found, jnp.zeros((1, t), F32), jnp.full((1, t), nsel_f, F32)))

    no_tie = jnp.min(found) == 1.0

    @pl.when(no_tie)
    def _():
        def body(jj, carry):
            sc_ref[span(jj), :] = jnp.where(sc_ref[span(jj), :] >= thr, 0.0, MASK_LOGIT)
            return carry
        lax.fori_loop(0, n_span, body, 0)

    @pl.when(jnp.logical_not(no_tie))
    def _():
        found2 = found > 0.0
        cut = jnp.where(found2, thr, jnp.where(zero_cut, 0.0, tau))
        need = jnp.where(found2, jnp.float32(2 * ATT_BLOCK + 2),
                         nsel_f - jnp.where(zero_cut, n_pos, above))

        def body(jj, seen):
            s = sc_ref[span(jj), :]
            eq = s == cut
            eqf = jnp.where(eq, 1.0, 0.0)
            eqb = eqf.astype(BF16)
            first = jnp.sum(eqf[:t], axis=0, keepdims=True)
            rank = jnp.concatenate(
                [seen + jnp.dot(tri_ref[...], eqb[:t], preferred_element_type=F32),
                 seen + first + jnp.dot(tri_ref[...], eqb[t:], preferred_element_type=F32)],
                axis=0)
            keep = (s > cut) | (eq & (rank < need))
            sc_ref[span(jj), :] = jnp.where(keep, 0.0, MASK_LOGIT)
            total = first + jnp.sum(eqf[t:], axis=0, keepdims=True)
            return seen + jnp.where(found2, 0.0, total)
        lax.fori_loop(0, n_span, body, jnp.zeros((1, t), F32))

    load_masked_heads(q_ref)
    _flash_init(m_ref, l_ref, acc_ref)

    v_rows = DSA_HEAD_DIM + (SUM_ROWS if mxu_sum else 0)
    per_group = DSA_HEADS // att_groups
    groups = [range(g * per_group, (g + 1) * per_group) for g in range(att_groups)]

    def logits(j, heads):
        for h in heads:
            cols = slice(128 * (h // 2), 128 * (h // 2 + 1))
            s_ref[h] = jnp.dot(k_ref[0, _chunk(j), cols], qz_ref[h],
                               preferred_element_type=F32)

    def softmax_pv(j, heads, near):
        for h in heads:
            rows = slice(h * v_rows, (h + 1) * v_rows)
            s = s_ref[h] + sc_ref[_chunk(j), :]
            if near is not None:
                s = s + bias_ref[near, h]
            _flash_update(m_ref, l_ref, acc_ref, h, s, vt_ref[0, rows, _chunk(j)], mxu_sum)

    def step(j, carry, near=None, last=False):
        if att_groups == 1:
            logits(j, groups[0])
            softmax_pv(j, groups[0], near)
            return carry
        for g in range(att_groups):
            if g + 1 < att_groups:
                logits(j, groups[g + 1])
            elif not last:
                logits(j + 1, groups[0])
            softmax_pv(j, groups[g], near)
        return carry

    if att_groups > 1:
        logits(0, groups[0])
    lax.fori_loop(0, i - 1, step, 0)
    lax.fori_loop(jnp.maximum(i - 1, 0), i, functools.partial(step, near=1), 0)
    step(i, 0, near=0, last=True)

    for p in range(DSA_HEADS // 2):
        o = jnp.concatenate([_normalised(l_ref, acc_ref, h, DSA_HEAD_DIM, mxu_sum)
                             for h in (2 * p, 2 * p + 1)], axis=0)
        out_ref[0, :, 128 * p:128 * (p + 1)] = o.T.astype(out_ref.dtype)


def _dsa(main, iwt, ik2, vt, tab, n_sel, score_groups=1, att_groups=ATT_GROUPS,
         first_steps=SEARCH_FIRST_STEPS, mxu_sum=False):
    b, s, _ = main.shape
    t = ATT_BLOCK
    v_rows = DSA_HEAD_DIM + (SUM_ROWS if mxu_sum else 0)
    if mxu_sum:
        vt = _augment_vt(vt, DSA_HEADS)
    assert s % (2 * t) == 0, "the score passes walk two key chunks at a time"
    col = lambda c: (lambda bi, i: (bi, i, c))
    full = lambda c: (lambda bi, i: (bi, 0, c))
    return pl.pallas_call(
        functools.partial(_dsa_kernel, n_sel, score_groups, att_groups, first_steps, mxu_sum),
        grid=(b, s // t),
        in_specs=[
            pl.BlockSpec(memory_space=pltpu.SMEM),
            pl.BlockSpec((1, t, HEAD_COLS), col(3)),
            pl.BlockSpec((1, IDX_HEADS, t), lambda bi, i: (bi, 0, i)),
            pl.BlockSpec((1, s, 2 * IDX_DIM), full(0)),
            pl.BlockSpec((1, t, HEAD_COLS), col(0)),
            pl.BlockSpec((1, s, HEAD_COLS), full(1)),
            pl.BlockSpec((1, DSA_HEADS * v_rows, s), lambda bi, i: (bi, 0, 0)),
        ],
        out_specs=pl.BlockSpec((1, t, HEAD_COLS), col(0)),
        out_shape=jax.ShapeDtypeStruct((b, s, HEAD_COLS), BF16),
        scratch_shapes=[
            pltpu.VMEM((s, t), F32),
            pltpu.VMEM((2, DSA_HEADS, t, t), F32),
            pltpu.VMEM((t, t), BF16),
            pltpu.VMEM((DSA_HEADS, 128, t), BF16),
            pltpu.VMEM((DSA_HEADS, 1, t), F32),
            pltpu.VMEM((DSA_HEADS, 1, t), F32),
            pltpu.VMEM((DSA_HEADS, v_rows, t), F32),
            pltpu.VMEM((DSA_HEADS, t, t), F32),
        ],
        compiler_params=pltpu.CompilerParams(
            dimension_semantics=("arbitrary", "arbitrary"), vmem_limit_bytes=VMEM_LIMIT),
        name="dsa",
    )(tab, main, iwt, ik2, main, main, vt)


def _diff_kernel(mxu_sum, tab_ref, lam_ref, init_ref, subln_ref, q_ref, k_ref, vt_ref, out_ref,
                 bias_ref, qz_ref, m_ref, l_ref, acc_ref, s_ref):
    t = ATT_BLOCK
    i = pl.program_id(1)
    n_maps = 2 * DIFF_HEADS

    @pl.when((pl.program_id(0) == 0) & (i == 0))
    def _():
        _build_rel_bias(tab_ref, bias_ref, DSA_HEADS, DIFF_HEADS)

    key = lax.broadcasted_iota(jnp.int32, (t, t), 0)
    query = lax.broadcasted_iota(jnp.int32, (t, t), 1)
    causal_bias = jnp.where(key <= query, 0.0, MASK_LOGIT)

    q = q_ref[0]
    for m in range(n_maps):
        pair = q[:, 128 * (m // 2):128 * (m // 2 + 1)]
        qz_ref[m] = jnp.where(_half_lane_mask(m % 2), pair, jnp.zeros_like(pair)).T

    lam = lam_ref[...]
    lam_init = init_ref[0:1, 0:1]
    lam_full = (jnp.exp(jnp.sum(lam[0:1] * lam[1:2], axis=-1, keepdims=True))
                - jnp.exp(jnp.sum(lam[2:3] * lam[3:4], axis=-1, keepdims=True)) + lam_init)

    _flash_init(m_ref, l_ref, acc_ref)

    v_rows = DIFF_V_DIM + (SUM_ROWS if mxu_sum else 0)
    per_group = n_maps // ATT_GROUPS
    groups = [range(g * per_group, (g + 1) * per_group) for g in range(ATT_GROUPS)]

    def logits(j, maps):
        for m in maps:
            cols = slice(128 * (m // 2), 128 * (m // 2 + 1))
            s_ref[m] = jnp.dot(k_ref[0, _chunk(j), cols], qz_ref[m],
                               preferred_element_type=F32)

    def softmax_pv(j, maps, near):
        for m in maps:
            rows = slice(v_rows * (m // 2), v_rows * (m // 2 + 1))
            s = s_ref[m]
            if near is not None:
                s = s + bias_ref[near, m // 2]
            if near == 0:
                s = s + causal_bias
            _flash_update(m_ref, l_ref, acc_ref, m, s, vt_ref[0, rows, _chunk(j)], mxu_sum)

    def step(j, carry, near=None, last=False):
        for g in range(ATT_GROUPS):
            if g + 1 < ATT_GROUPS:
                logits(j, groups[g + 1])
            elif not last:
                logits(j + 1, groups[0])
            softmax_pv(j, groups[g], near)
        return carry

    logits(0, groups[0])
    lax.fori_loop(0, i - 1, step, 0)
    lax.fori_loop(jnp.maximum(i - 1, 0), i, functools.partial(step, near=1), 0)
    step(i, 0, near=0, last=True)

    for h in range(DIFF_HEADS):
        cols = slice(128 * h, 128 * (h + 1))
        maps = [_normalised(l_ref, acc_ref, 2 * h + c, DIFF_V_DIM, mxu_sum) for c in range(2)]
        o = maps[0] - lam_full * maps[1]
        o = o * lax.rsqrt(jnp.mean(o * o, axis=0, keepdims=True) + EPS) * subln_ref[...]
        o = o * (1.0 - lam_init)
        out_ref[0, :, cols] = o.T.astype(out_ref.dtype)


def _diff(main, vt, tab, lam, init_row, subln_col, mxu_sum=False):
    b, s, _ = main.shape
    t = ATT_BLOCK
    v_rows = DIFF_V_DIM + (SUM_ROWS if mxu_sum else 0)
    if mxu_sum:
        vt = _augment_vt(vt, DIFF_HEADS)
    col = lambda c: (lambda bi, i: (bi, i, c))
    full = lambda c: (lambda bi, i: (bi, 0, c))
    const = lambda bi, i: (0, 0)
    return pl.pallas_call(
        functools.partial(_diff_kernel, mxu_sum),
        grid=(b, s // t),
        in_specs=[
            pl.BlockSpec(memory_space=pltpu.SMEM),
            pl.BlockSpec((4, DIFF_HEAD_DIM), const),
            pl.BlockSpec((1, 128), const),
            pl.BlockSpec((DIFF_V_DIM, 1), const),
            pl.BlockSpec((1, t, HEAD_COLS), col(4)),
            pl.BlockSpec((1, s, HEAD_COLS), full(5)),
            pl.BlockSpec((1, DIFF_HEADS * v_rows, s), lambda bi, i: (bi, 0, 0)),
        ],
        out_specs=pl.BlockSpec((1, t, HEAD_COLS), col(0)),
        out_shape=jax.ShapeDtypeStruct((b, s, HEAD_COLS), BF16),
        scratch_shapes=[
            pltpu.VMEM((2, DIFF_HEADS, t, t), F32),
            pltpu.VMEM((2 * DIFF_HEADS, 128, t), BF16),
            pltpu.VMEM((2 * DIFF_HEADS, 1, t), F32),
            pltpu.VMEM((2 * DIFF_HEADS, 1, t), F32),
            pltpu.VMEM((2 * DIFF_HEADS, v_rows, t), F32),
            pltpu.VMEM((2 * DIFF_HEADS, t, t), F32),
        ],
        compiler_params=pltpu.CompilerParams(
            dimension_semantics=("arbitrary", "arbitrary"), vmem_limit_bytes=VMEM_LIMIT),
        name="diff",
    )(tab, lam, init_row, subln_col, main, main, vt)


def _ffn_kernel(final, ff_chunk,
                x_ref, a_ref, b_ref, wo_ref, g_ref, wg_ref, wu_ref, wd_ref, gf_ref,
                out_ref, h_ref, y_ref):
    half = wo_ref.shape[0] // 2
    x1 = (x_ref[...]
          + jnp.dot(a_ref[...], wo_ref[:half, :], preferred_element_type=F32)
          + jnp.dot(b_ref[...], wo_ref[half:, :], preferred_element_type=F32))
    h = x1 * lax.rsqrt(jnp.mean(x1 * x1, axis=-1, keepdims=True) + EPS) * g_ref[...]
    h_ref[...] = h.astype(BF16)
    y_ref[...] = x1
    d_ff = wg_ref.shape[1]

    def body(f, carry):
        cols = pl.ds(pl.multiple_of(f * ff_chunk, ff_chunk), ff_chunk)
        hb = h_ref[...]
        g = jnp.dot(hb, wg_ref[:, cols], preferred_element_type=F32)
        u = jnp.dot(hb, wu_ref[:, cols], preferred_element_type=F32)
        act = (g / (1.0 + jnp.exp(-g)) * u).astype(BF16)
        y_ref[...] += jnp.dot(act, wd_ref[cols, :], preferred_element_type=F32)
        return carry

    lax.fori_loop(0, d_ff // ff_chunk, body, 0)
    y = y_ref[...]
    if final:
        y = y * lax.rsqrt(jnp.mean(y * y, axis=-1, keepdims=True) + EPS) * gf_ref[...]
    out_ref[...] = y


def _ffn(x2, a, b, wo, g, wg, wu, wd, gf, final, row_block=ROW_BLOCK, ff_chunk=FF_CHUNK,
         single_buffer=False):
    t, d = x2.shape
    d_ff = wg.shape[1]
    assert d_ff % ff_chunk == 0
    row = lambda i: (i, 0)
    const = lambda i: (0, 0)
    return pl.pallas_call(
        functools.partial(_ffn_kernel, final, ff_chunk),
        grid=(t // row_block,),
        in_specs=[
            pl.BlockSpec((row_block, d), row),
            pl.BlockSpec((row_block, HEAD_COLS), row),
            pl.BlockSpec((row_block, HEAD_COLS), row),
            _resident((2 * HEAD_COLS, d), const, single_buffer),
            pl.BlockSpec((1, d), const),
            _resident((d, d_ff), const, single_buffer),
            _resident((d, d_ff), const, single_buffer),
            _resident((d_ff, d), const, single_buffer),
            pl.BlockSpec((1, d), const),
        ],
        out_specs=pl.BlockSpec((row_block, d), row),
        out_shape=jax.ShapeDtypeStruct((t, d), F32),
        scratch_shapes=[
            pltpu.VMEM((row_block, d), BF16),
            pltpu.VMEM((row_block, d), F32),
        ],
        compiler_params=pltpu.CompilerParams(
            dimension_semantics=("arbitrary",), vmem_limit_bytes=VMEM_LIMIT),
        name="ffn",
    )(x2, a, b, wo, g, wg, wu, wd, gf)


def _pack_w_in(w):
    hd = DSA_HEADS * DSA_HEAD_DIM
    off = 0
    parts = {}
    for name, width in (("dq", hd), ("dk", hd), ("dv", hd), ("iq", IDX_HEADS * IDX_DIM),
                        ("ik", IDX_DIM), ("iw", IDX_HEADS),
                        ("fq", 2 * DIFF_HEADS * DIFF_HEAD_DIM),
                        ("fk", 2 * DIFF_HEADS * DIFF_HEAD_DIM),
                        ("fv", DIFF_HEADS * DIFF_V_DIM)):
        parts[name] = w[:, off:off + width]
        off += width
    pad = jnp.zeros((w.shape[0], N_AUX - IDX_DIM - IDX_HEADS), w.dtype)
    return jnp.concatenate(
        [parts["dq"] * (DSA_HEAD_DIM ** -0.5 * LOG2E), parts["dk"], parts["dv"], parts["iq"],
         parts["fq"] * (DIFF_HEAD_DIM ** -0.5 * LOG2E), parts["fk"], parts["fv"],
         parts["ik"], parts["iw"], pad], axis=1).astype(BF16)


AB_VARIANTS = (
    dict(dsa=dict(first_steps=16), diff=dict(), ffn=dict(ff_chunk=1408)),
    dict(dsa=dict(first_steps=18), diff=dict(mxu_sum=True), ffn=dict(ff_chunk=2816)),
    dict(dsa=dict(first_steps=14), diff=dict(), ffn=dict(ff_chunk=1408)),
    dict(dsa=dict(first_steps=16, mxu_sum=True), diff=dict(), ffn=dict(ff_chunk=1408)),
)


def kernel(x, attn_norm, w_in, diff_lambda, diff_subln, w_out, ffn_norm,
           w_gate, w_up, w_down, rel_bias, final_norm):
    b, s, d = x.shape
    depth = w_in.shape[0]
    n_sel = min(TOPK_MAX, s // 4)
    x2 = x.reshape(b * s, d)
    for l in range(depth):
        lam_init = 0.8 - 0.6 * math.exp(-0.3 * l)
        ab = AB_VARIANTS[l % len(AB_VARIANTS)]
        main, aux = _inproj(x2, attn_norm[l][None, :], _pack_w_in(w_in[l]))
        main = main.reshape(b, s, N_MAIN)
        aux = aux.reshape(b, s, N_AUX)
        ik = aux[:, :, :IDX_DIM].astype(BF16)
        ik2 = jnp.concatenate([ik, ik], axis=-1)
        iwt = jnp.swapaxes(aux[:, :, IDX_DIM:IDX_DIM + IDX_HEADS], 1, 2)
        dvt = jnp.swapaxes(main[:, :, 2 * HEAD_COLS:3 * HEAD_COLS], 1, 2)
        fvt = jnp.swapaxes(main[:, :, 6 * HEAD_COLS:7 * HEAD_COLS], 1, 2)
        dsa_out = _dsa(main, iwt, ik2, dvt, rel_bias, n_sel, **ab["dsa"])
        diff_out = _diff(main, fvt, rel_bias, diff_lambda[l],
                         jnp.full((1, 128), lam_init, F32), diff_subln[l][:, None], **ab["diff"])
        x2 = _ffn(x2, dsa_out.reshape(b * s, HEAD_COLS), diff_out.reshape(b * s, HEAD_COLS),
                  w_out[l].astype(BF16), ffn_norm[l][None, :], w_gate[l].astype(BF16),
                  w_up[l].astype(BF16), w_down[l].astype(BF16), final_norm[None, :],
                  final=(l == depth - 1), **ab["ffn"])
    return x2.reshape(b, s, d)
```

```python
import functools
import math

import jax
import jax.numpy as jnp
from jax import lax
from jax.experimental import pallas as pl
from jax.experimental.pallas import tpu as pltpu

DSA_HEADS = 8
DSA_HEAD_DIM = 64
IDX_HEADS = 8
IDX_DIM = 64
TOPK_MAX = 256
DIFF_HEADS = 4
DIFF_HEAD_DIM = 64
DIFF_V_DIM = 2 * DIFF_HEAD_DIM
NUM_BUCKETS = 32
MAX_DISTANCE = 128
EPS = 1e-6

HEAD_COLS = 512
N_MAIN = 7 * HEAD_COLS
N_AUX = 128
ATT_BLOCK = 256
ROW_BLOCK = 512
IN_ROW_BLOCK = 1024

MASK_SCORE = -3.0e38
SELECT_ALL = -1.0e38
MASK_LOGIT = -1.0e30
LOG2E = math.log2(math.e)
SUM_ROWS = 16
ATT_GROUPS = 2
SEARCH_FIRST_STEPS = 18
SEARCH_EXTRA_ROUNDS = 11
VMEM_LIMIT = 56 * 1024 * 1024

F32 = jnp.float32
BF16 = jnp.bfloat16


def _inproj_kernel(x_ref, g_ref, w_ref, main_ref, aux_ref):
    x = x_ref[...]
    h = x * lax.rsqrt(jnp.mean(x * x, axis=-1, keepdims=True) + EPS) * g_ref[...]
    h = h.astype(BF16)
    for c in range(0, N_MAIN, HEAD_COLS):
        main_ref[:, c:c + HEAD_COLS] = jnp.dot(
            h, w_ref[:, c:c + HEAD_COLS], preferred_element_type=F32).astype(BF16)
    aux_ref[...] = jnp.dot(h, w_ref[:, N_MAIN:], preferred_element_type=F32)


def _inproj(x2, g, w_all):
    t, d = x2.shape
    rows = IN_ROW_BLOCK
    return pl.pallas_call(
        _inproj_kernel,
        grid=(t // rows,),
        in_specs=[
            pl.BlockSpec((rows, d), lambda i: (i, 0)),
            pl.BlockSpec((1, d), lambda i: (0, 0)),
            pl.BlockSpec((d, N_MAIN + N_AUX), lambda i: (0, 0)),
        ],
        out_specs=[
            pl.BlockSpec((rows, N_MAIN), lambda i: (i, 0)),
            pl.BlockSpec((rows, N_AUX), lambda i: (i, 0)),
        ],
        out_shape=[
            jax.ShapeDtypeStruct((t, N_MAIN), BF16),
            jax.ShapeDtypeStruct((t, N_AUX), F32),
        ],
        compiler_params=pltpu.CompilerParams(
            dimension_semantics=("arbitrary",), vmem_limit_bytes=VMEM_LIMIT),
        name="inproj",
    )(x2, g, w_all)


def _build_rel_bias(tab_ref, bias_near_ref, head0, n_heads):
    t = ATT_BLOCK
    key = lax.broadcasted_iota(jnp.int32, (t, t), 0)
    query = lax.broadcasted_iota(jnp.int32, (t, t), 1)
    max_exact = NUM_BUCKETS // 2
    for d in range(2):
        n = jnp.maximum(query - key + d * t, 0)
        nf = jnp.maximum(n, 1).astype(F32)
        large = max_exact + (jnp.log(nf / max_exact) / math.log(MAX_DISTANCE / max_exact)
                             * (NUM_BUCKETS - max_exact)).astype(jnp.int32)
        large = jnp.minimum(large, NUM_BUCKETS - 1)
        bucket = jnp.where(n < max_exact, n, large)
        for h in range(n_heads):
            bias_near_ref[d, h] = jnp.zeros((t, t), F32)

        def fill(b, carry, d=d, bucket=bucket):
            hit = bucket == b
            for h in range(n_heads):
                val = (tab_ref[b, head0 + h] - tab_ref[NUM_BUCKETS - 1, head0 + h]) * LOG2E
                bias_near_ref[d, h] = jnp.where(hit, val, bias_near_ref[d, h])
            return carry

        lax.fori_loop(0, NUM_BUCKETS - 1, fill, 0)


def _chunk(j):
    return pl.ds(pl.multiple_of(j * ATT_BLOCK, ATT_BLOCK), ATT_BLOCK)


def _half_lane_mask(half):
    lane = lax.broadcasted_iota(jnp.int32, (ATT_BLOCK, 128), 1)
    return (lane >= 64 * half) & (lane < 64 * (half + 1))


def _flash_update(m_ref, acc_ref, idx, s, vt):
    m = m_ref[idx]
    m_new = jnp.maximum(m, jnp.max(s, axis=0, keepdims=True))
    alpha = jnp.exp2(m - m_new)
    p = jnp.exp2(s - m_new)
    acc_ref[idx] = alpha * acc_ref[idx] + jnp.dot(vt, p.astype(BF16), preferred_element_type=F32)
    m_ref[idx] = m_new


def _flash_init(m_ref, acc_ref):
    m_ref[...] = jnp.full(m_ref.shape, MASK_LOGIT, F32)
    acc_ref[...] = jnp.zeros(acc_ref.shape, F32)


def _normalised(acc_ref, idx, dv):
    acc = acc_ref[idx]
    return acc[:dv] / acc[dv:dv + 1]


def _augment_vt(vt, heads):
    b, hd, s = vt.shape
    dv = hd // heads
    v4 = vt.reshape(b, heads, dv, s)
    ones = jnp.ones((b, heads, 1, s), vt.dtype)
    zeros = jnp.zeros((b, heads, SUM_ROWS - 1, s), vt.dtype)
    return jnp.concatenate([v4, ones, zeros], axis=2).reshape(b, heads * (dv + SUM_ROWS), s)


def _pipelined_steps(i, n_maps, logits, softmax_pv):
    per_group = n_maps // ATT_GROUPS
    groups = [range(g * per_group, (g + 1) * per_group) for g in range(ATT_GROUPS)]

    def step(j, carry, near=None, last=False):
        for g in range(ATT_GROUPS):
            if g + 1 < ATT_GROUPS:
                logits(j, groups[g + 1])
            elif not last:
                logits(j + 1, groups[0])
            softmax_pv(j, groups[g], near)
        return carry

    logits(0, groups[0])
    lax.fori_loop(0, i - 1, step, 0)
    lax.fori_loop(jnp.maximum(i - 1, 0), i, functools.partial(step, near=1), 0)
    step(i, 0, near=0, last=True)


def _dsa_kernel(n_sel, tab_ref, iq_ref, iwt_ref, ik_ref, q_ref, k_ref, vt_ref, out_ref,
                sc_ref, bias_ref, tri_ref, qz_ref, m_ref, acc_ref, s_ref):
    t = ATT_BLOCK
    i = pl.program_id(1)

    @pl.when((pl.program_id(0) == 0) & (i == 0))
    def _():
        _build_rel_bias(tab_ref, bias_ref, 0, DSA_HEADS)
        r = lax.broadcasted_iota(jnp.int32, (t, t), 0)
        c = lax.broadcasted_iota(jnp.int32, (t, t), 1)
        tri_ref[...] = jnp.where(c < r, 1.0, 0.0).astype(BF16)

    key = lax.broadcasted_iota(jnp.int32, (t, t), 0)
    query = lax.broadcasted_iota(jnp.int32, (t, t), 1)
    causal = key <= query
    nsel_f = jnp.float32(n_sel)
    big = jnp.float32(3.0e38)

    def load_masked_heads(src_ref):
        x = src_ref[0]
        for h in range(DSA_HEADS):
            pair = x[:, 128 * (h // 2):128 * (h // 2 + 1)]
            qz_ref[h] = jnp.where(_half_lane_mask(h % 2), pair, jnp.zeros_like(pair)).T

    load_masked_heads(iq_ref)
    w = iwt_ref[0] * (IDX_HEADS ** -0.5 * IDX_DIM ** -0.5)

    def score_tile(j):
        kc = ik_ref[0, _chunk(j), :]
        acc = jnp.zeros((t, t), F32)
        for h in range(IDX_HEADS):
            d = jnp.dot(kc, qz_ref[h], preferred_element_type=F32)
            acc = acc + w[h:h + 1, :] * jnp.maximum(d, 0.0)
        return acc

    def score_body(j, carry):
        lo, hi = carry
        acc = score_tile(j)
        sc_ref[_chunk(j), :] = acc
        return (jnp.minimum(lo, jnp.min(acc, axis=0, keepdims=True)),
                jnp.maximum(hi, jnp.max(acc, axis=0, keepdims=True)))

    lo, hi = lax.fori_loop(0, i, score_body,
                           (jnp.full((1, t), big, F32), jnp.full((1, t), -big, F32)))
    acc = score_tile(i)
    sc_ref[_chunk(i), :] = jnp.where(causal, acc, MASK_SCORE)
    lo = jnp.minimum(lo, jnp.min(jnp.where(causal, acc, big), axis=0, keepdims=True))
    hi = jnp.maximum(hi, jnp.max(jnp.where(causal, acc, -big), axis=0, keepdims=True))
    hi = hi + jnp.maximum(jnp.abs(hi), 1.0e-30)

    @pl.when(i % 2 == 0)
    def _():
        sc_ref[_chunk(i + 1), :] = jnp.full((t, t), MASK_SCORE, F32)

    n_span = i // 2 + 1

    def span(jj):
        return pl.ds(pl.multiple_of(jj * (2 * t), 2 * t), 2 * t)

    def fold(x):
        return x.reshape(x.shape[0] // 8, 8, t)

    def count(pred_fn):
        def body(jj, acc):
            for r in range(0, 2 * t, t // 2):
                rows = pl.ds(pl.multiple_of(jj * (2 * t) + r, t // 2), t // 2)
                ind = jnp.where(pred_fn(sc_ref[rows, :]), 1.0, 0.0)
                acc = acc + jnp.sum(fold(ind), axis=0)
            return acc
        part = lax.fori_loop(0, n_span, body, jnp.zeros((8, t), F32))
        return jnp.sum(part, axis=0, keepdims=True)

    def min_at_least(thr):
        def body(jj, acc):
            s = sc_ref[span(jj), :]
            return jnp.minimum(acc, jnp.min(fold(jnp.where(s >= thr, s, big)), axis=0))
        part = lax.fori_loop(0, n_span, body, jnp.full((8, t), big, F32))
        return jnp.min(part, axis=0, keepdims=True)

    pos = i * t + lax.broadcasted_iota(jnp.int32, (1, t), 1)
    small = pos < n_sel

    def count_pos_zero():
        def body(jj, acc):
            s = sc_ref[span(jj), :]
            return (acc[0] + jnp.sum(fold(jnp.where(s > 0.0, 1.0, 0.0)), axis=0),
                    acc[1] + jnp.sum(fold(jnp.where(s == 0.0, 1.0, 0.0)), axis=0))
        zero = jnp.zeros((8, t), F32)
        a, b = lax.fori_loop(0, n_span, body, (zero, zero))
        return jnp.sum(a, axis=0, keepdims=True), jnp.sum(b, axis=0, keepdims=True)

    n_pos, n_zero = count_pos_zero()
    zero_cut = (n_pos < nsel_f) & (n_pos + n_zero >= nsel_f) & jnp.logical_not(small)
    zflag = jnp.where(zero_cut, 1.0, 0.0)
    lo = jnp.where(n_pos >= nsel_f, 0.0, lo)
    hi = jnp.where(n_pos + n_zero < nsel_f, 0.0, hi)

    def bisect(st):
        lo, hi, thr, found = st
        mid = 0.5 * lo + 0.5 * hi
        cnt = count(lambda s: s >= mid)
        up = cnt >= nsel_f
        hit = cnt == nsel_f
        thr = jnp.where(hit & (found == 0.0), mid, thr)
        found = jnp.where(hit, 1.0, found)
        return (jnp.where(up, mid, lo), jnp.where(up, hi, mid), thr, found)

    def bisect_n(n, st):
        return lax.fori_loop(0, n, lambda _, s: bisect(s), st)

    def unresolved(flag):
        return jnp.min(flag) == 0.0

    def search_cond(st):
        it, _, _, _, found = st
        return (it < SEARCH_EXTRA_ROUNDS) & unresolved(jnp.maximum(found, zflag))

    def search_body(st):
        it, lo, hi, thr, found = st
        lo, hi, thr, found = bisect_n(2, (lo, hi, thr, found))
        return (it + 1, lo, hi, thr, found)

    thr0 = jnp.full((1, t), SELECT_ALL, F32)
    st = bisect_n(SEARCH_FIRST_STEPS, (lo, hi, thr0, jnp.where(small, 1.0, 0.0)))
    _, lo, hi, thr, found = lax.while_loop(search_cond, search_body, (jnp.int32(0),) + st)

    def tie_cond(st):
        _, _, _, found, _, above = st
        return unresolved(jnp.maximum(jnp.maximum(found, zflag),
                                      jnp.where(above < nsel_f, 1.0, 0.0)))

    def tie_body(st):
        lo, hi, thr, found, _, _ = st
        lo, hi, thr, found = bisect_n(4, (lo, hi, thr, found))
        tau = min_at_least(lo)
        return (lo, hi, thr, found, tau, count(lambda s: s > tau))

    _, _, thr, found, tau, above = lax.while_loop(
        tie_cond, tie_body,
        (lo, hi, thr, found, jnp.zeros((1, t), F32), jnp.full((1, t), nsel_f, F32)))

    no_tie = jnp.min(found) == 1.0

    @pl.when(no_tie)
    def _():
        def body(jj, carry):
            sc_ref[span(jj), :] = jnp.where(sc_ref[span(jj), :] >= thr, 0.0, MASK_LOGIT)
            return carry
        lax.fori_loop(0, n_span, body, 0)

    @pl.when(jnp.logical_not(no_tie))
    def _():
        found2 = found > 0.0
        cut = jnp.where(found2, thr, jnp.where(zero_cut, 0.0, tau))
        need = jnp.where(found2, jnp.float32(2 * ATT_BLOCK + 2),
                         nsel_f - jnp.where(zero_cut, n_pos, above))

        def body(jj, seen):
            s = sc_ref[span(jj), :]
            eq = s == cut
            eqf = jnp.where(eq, 1.0, 0.0)
            eqb = eqf.astype(BF16)
            first = jnp.sum(eqf[:t], axis=0, keepdims=True)
            rank = jnp.concatenate(
                [seen + jnp.dot(tri_ref[...], eqb[:t], preferred_element_type=F32),
                 seen + first + jnp.dot(tri_ref[...], eqb[t:], preferred_element_type=F32)],
                axis=0)
            keep = (s > cut) | (eq & (rank < need))
            sc_ref[span(jj), :] = jnp.where(keep, 0.0, MASK_LOGIT)
            total = first + jnp.sum(eqf[t:], axis=0, keepdims=True)
            return seen + jnp.where(found2, 0.0, total)
        lax.fori_loop(0, n_span, body, jnp.zeros((1, t), F32))

    load_masked_heads(q_ref)
    _flash_init(m_ref, acc_ref)
    v_rows = DSA_HEAD_DIM + SUM_ROWS

    def logits(j, heads):
        for h in heads:
            cols = slice(128 * (h // 2), 128 * (h // 2 + 1))
            s_ref[h] = jnp.dot(k_ref[0, _chunk(j), cols], qz_ref[h],
                               preferred_element_type=F32)

    def softmax_pv(j, heads, near):
        for h in heads:
            rows = slice(h * v_rows, (h + 1) * v_rows)
            s = s_ref[h] + sc_ref[_chunk(j), :]
            if near is not None:
                s = s + bias_ref[near, h]
            _flash_update(m_ref, acc_ref, h, s, vt_ref[0, rows, _chunk(j)])

    _pipelined_steps(i, DSA_HEADS, logits, softmax_pv)

    for p in range(DSA_HEADS // 2):
        o = jnp.concatenate([_normalised(acc_ref, h, DSA_HEAD_DIM)
                             for h in (2 * p, 2 * p + 1)], axis=0)
        out_ref[0, :, 128 * p:128 * (p + 1)] = o.T.astype(out_ref.dtype)


def _dsa(main, iwt, ik2, vt, tab, n_sel):
    b, s, _ = main.shape
    t = ATT_BLOCK
    assert s % (2 * t) == 0, "the score passes walk two key chunks at a time"
    v_rows = DSA_HEAD_DIM + SUM_ROWS
    col = lambda c: (lambda bi, i: (bi, i, c))
    full = lambda c: (lambda bi, i: (bi, 0, c))
    return pl.pallas_call(
        functools.partial(_dsa_kernel, n_sel),
        grid=(b, s // t),
        in_specs=[
            pl.BlockSpec(memory_space=pltpu.SMEM),
            pl.BlockSpec((1, t, HEAD_COLS), col(3)),
            pl.BlockSpec((1, IDX_HEADS, t), lambda bi, i: (bi, 0, i)),
            pl.BlockSpec((1, s, 2 * IDX_DIM), full(0)),
            pl.BlockSpec((1, t, HEAD_COLS), col(0)),
            pl.BlockSpec((1, s, HEAD_COLS), full(1)),
            pl.BlockSpec((1, DSA_HEADS * v_rows, s), lambda bi, i: (bi, 0, 0)),
        ],
        out_specs=pl.BlockSpec((1, t, HEAD_COLS), col(0)),
        out_shape=jax.ShapeDtypeStruct((b, s, HEAD_COLS), BF16),
        scratch_shapes=[
            pltpu.VMEM((s, t), F32),
            pltpu.VMEM((2, DSA_HEADS, t, t), F32),
            pltpu.VMEM((t, t), BF16),
            pltpu.VMEM((DSA_HEADS, 128, t), BF16),
            pltpu.VMEM((DSA_HEADS, 1, t), F32),
            pltpu.VMEM((DSA_HEADS, v_rows, t), F32),
            pltpu.VMEM((DSA_HEADS, t, t), F32),
        ],
        compiler_params=pltpu.CompilerParams(
            dimension_semantics=("arbitrary", "arbitrary"), vmem_limit_bytes=VMEM_LIMIT),
        name="dsa",
    )(tab, main, iwt, ik2, main, main, _augment_vt(vt, DSA_HEADS))


def _diff_kernel(tab_ref, lam_ref, init_ref, subln_ref, q_ref, k_ref, vt_ref, out_ref,
                 bias_ref, qz_ref, m_ref, acc_ref, s_ref):
    t = ATT_BLOCK
    i = pl.program_id(1)
    n_maps = 2 * DIFF_HEADS

    @pl.when((pl.program_id(0) == 0) & (i == 0))
    def _():
        _build_rel_bias(tab_ref, bias_ref, DSA_HEADS, DIFF_HEADS)

    key = lax.broadcasted_iota(jnp.int32, (t, t), 0)
    query = lax.broadcasted_iota(jnp.int32, (t, t), 1)
    causal_bias = jnp.where(key <= query, 0.0, MASK_LOGIT)

    q = q_ref[0]
    for m in range(n_maps):
        pair = q[:, 128 * (m // 2):128 * (m // 2 + 1)]
        qz_ref[m] = jnp.where(_half_lane_mask(m % 2), pair, jnp.zeros_like(pair)).T

    lam = lam_ref[...]
    lam_init = init_ref[0:1, 0:1]
    lam_full = (jnp.exp(jnp.sum(lam[0:1] * lam[1:2], axis=-1, keepdims=True))
                - jnp.exp(jnp.sum(lam[2:3] * lam[3:4], axis=-1, keepdims=True)) + lam_init)

    _flash_init(m_ref, acc_ref)
    v_rows = DIFF_V_DIM + SUM_ROWS

    def logits(j, maps):
        for m in maps:
            cols = slice(128 * (m // 2), 128 * (m // 2 + 1))
            s_ref[m] = jnp.dot(k_ref[0, _chunk(j), cols], qz_ref[m],
                               preferred_element_type=F32)

    def softmax_pv(j, maps, near):
        for m in maps:
            rows = slice(v_rows * (m // 2), v_rows * (m // 2 + 1))
            s = s_ref[m]
            if near is not None:
                s = s + bias_ref[near, m // 2]
            if near == 0:
                s = s + causal_bias
            _flash_update(m_ref, acc_ref, m, s, vt_ref[0, rows, _chunk(j)])

    _pipelined_steps(i, n_maps, logits, softmax_pv)

    for h in range(DIFF_HEADS):
        cols = slice(128 * h, 128 * (h + 1))
        maps = [_normalised(acc_ref, 2 * h + c, DIFF_V_DIM) for c in range(2)]
        o = maps[0] - lam_full * maps[1]
        o = o * lax.rsqrt(jnp.mean(o * o, axis=0, keepdims=True) + EPS) * subln_ref[...]
        o = o * (1.0 - lam_init)
        out_ref[0, :, cols] = o.T.astype(out_ref.dtype)


def _diff(main, vt, tab, lam, init_row, subln_col):
    b, s, _ = main.shape
    t = ATT_BLOCK
    v_rows = DIFF_V_DIM + SUM_ROWS
    col = lambda c: (lambda bi, i: (bi, i, c))
    full = lambda c: (lambda bi, i: (bi, 0, c))
    const = lambda bi, i: (0, 0)
    return pl.pallas_call(
        _diff_kernel,
        grid=(b, s // t),
        in_specs=[
            pl.BlockSpec(memory_space=pltpu.SMEM),
            pl.BlockSpec((4, DIFF_HEAD_DIM), const),
            pl.BlockSpec((1, 128), const),
            pl.BlockSpec((DIFF_V_DIM, 1), const),
            pl.BlockSpec((1, t, HEAD_COLS), col(4)),
            pl.BlockSpec((1, s, HEAD_COLS), full(5)),
            pl.BlockSpec((1, DIFF_HEADS * v_rows, s), lambda bi, i: (bi, 0, 0)),
        ],
        out_specs=pl.BlockSpec((1, t, HEAD_COLS), col(0)),
        out_shape=jax.ShapeDtypeStruct((b, s, HEAD_COLS), BF16),
        scratch_shapes=[
            pltpu.VMEM((2, DIFF_HEADS, t, t), F32),
            pltpu.VMEM((2 * DIFF_HEADS, 128, t), BF16),
            pltpu.VMEM((2 * DIFF_HEADS, 1, t), F32),
            pltpu.VMEM((2 * DIFF_HEADS, v_rows, t), F32),
            pltpu.VMEM((2 * DIFF_HEADS, t, t), F32),
        ],
        compiler_params=pltpu.CompilerParams(
            dimension_semantics=("arbitrary", "arbitrary"), vmem_limit_bytes=VMEM_LIMIT),
        name="diff",
    )(tab, lam, init_row, subln_col, main, main, _augment_vt(vt, DIFF_HEADS))


def _ffn_kernel(final, x_ref, a_ref, b_ref, wo_ref, g_ref, wg_ref, wu_ref, wd_ref, gf_ref,
                out_ref):
    half = wo_ref.shape[0] // 2
    x1 = (x_ref[...]
          + jnp.dot(a_ref[...], wo_ref[:half, :], preferred_element_type=F32)
          + jnp.dot(b_ref[...], wo_ref[half:, :], preferred_element_type=F32))
    h = x1 * lax.rsqrt(jnp.mean(x1 * x1, axis=-1, keepdims=True) + EPS) * g_ref[...]
    hb = h.astype(BF16)
    g = jnp.dot(hb, wg_ref[...], preferred_element_type=F32)
    u = jnp.dot(hb, wu_ref[...], preferred_element_type=F32)
    act = (g / (1.0 + jnp.exp(-g)) * u).astype(BF16)
    y = x1 + jnp.dot(act, wd_ref[...], preferred_element_type=F32)
    if final:
        y = y * lax.rsqrt(jnp.mean(y * y, axis=-1, keepdims=True) + EPS) * gf_ref[...]
    out_ref[...] = y


def _ffn(x2, a, b, wo, g, wg, wu, wd, gf, final):
    t, d = x2.shape
    d_ff = wg.shape[1]
    row = lambda i: (i, 0)
    const = lambda i: (0, 0)
    return pl.pallas_call(
        functools.partial(_ffn_kernel, final),
        grid=(t // ROW_BLOCK,),
        in_specs=[
            pl.BlockSpec((ROW_BLOCK, d), row),
            pl.BlockSpec((ROW_BLOCK, HEAD_COLS), row),
            pl.BlockSpec((ROW_BLOCK, HEAD_COLS), row),
            pl.BlockSpec((2 * HEAD_COLS, d), const),
            pl.BlockSpec((1, d), const),
            pl.BlockSpec((d, d_ff), const),
            pl.BlockSpec((d, d_ff), const),
            pl.BlockSpec((d_ff, d), const),
            pl.BlockSpec((1, d), const),
        ],
        out_specs=pl.BlockSpec((ROW_BLOCK, d), row),
        out_shape=jax.ShapeDtypeStruct((t, d), F32),
        compiler_params=pltpu.CompilerParams(
            dimension_semantics=("arbitrary",), vmem_limit_bytes=VMEM_LIMIT),
        name="ffn",
    )(x2, a, b, wo, g, wg, wu, wd, gf)


def _pack_w_in(w):
    hd = DSA_HEADS * DSA_HEAD_DIM
    off = 0
    parts = {}
    for name, width in (("dq", hd), ("dk", hd), ("dv", hd), ("iq", IDX_HEADS * IDX_DIM),
                        ("ik", IDX_DIM), ("iw", IDX_HEADS),
                        ("fq", 2 * DIFF_HEADS * DIFF_HEAD_DIM),
                        ("fk", 2 * DIFF_HEADS * DIFF_HEAD_DIM),
                        ("fv", DIFF_HEADS * DIFF_V_DIM)):
        parts[name] = w[:, off:off + width]
        off += width
    pad = jnp.zeros((w.shape[0], N_AUX - IDX_DIM - IDX_HEADS), w.dtype)
    return jnp.concatenate(
        [parts["dq"] * (DSA_HEAD_DIM ** -0.5 * LOG2E), parts["dk"], parts["dv"], parts["iq"],
         parts["fq"] * (DIFF_HEAD_DIM ** -0.5 * LOG2E), parts["fk"], parts["fv"],
         parts["ik"], parts["iw"], pad], axis=1).astype(BF16)


def kernel(x, attn_norm, w_in, diff_lambda, diff_subln, w_out, ffn_norm,
           w_gate, w_up, w_down, rel_bias, final_norm):
    b, s, d = x.shape
    depth = w_in.shape[0]
    n_sel = min(TOPK_MAX, s // 4)
    x2 = x.reshape(b * s, d)
    for l in range(depth):
        lam_init = 0.8 - 0.6 * math.exp(-0.3 * l)
        main, aux = _inproj(x2, attn_norm[l][None, :], _pack_w_in(w_in[l]))
        main = main.reshape(b, s, N_MAIN)
        aux = aux.reshape(b, s, N_AUX)
        ik = aux[:, :, :IDX_DIM].astype(BF16)
        ik2 = jnp.concatenate([ik, ik], axis=-1)
        iwt = jnp.swapaxes(aux[:, :, IDX_DIM:IDX_DIM + IDX_HEADS], 1, 2)
        dvt = jnp.swapaxes(main[:, :, 2 * HEAD_COLS:3 * HEAD_COLS], 1, 2)
        fvt = jnp.swapaxes(main[:, :, 6 * HEAD_COLS:7 * HEAD_COLS], 1, 2)
        dsa_out = _dsa(main, iwt, ik2, dvt, rel_bias, n_sel)
        diff_out = _diff(main, fvt, rel_bias, diff_lambda[l],
                         jnp.full((1, 128), lam_init, F32), diff_subln[l][:, None])
        x2 = _ffn(x2, dsa_out.reshape(b * s, HEAD_COLS), diff_out.reshape(b * s, HEAD_COLS),
                  w_out[l].astype(BF16), ffn_norm[l][None, :], w_gate[l].astype(BF16),
                  w_up[l].astype(BF16), w_down[l].astype(BF16), final_norm[None, :],
                  final=(l == depth - 1))
    return x2.reshape(b, s, d)
```

```python
import functools
import math

import jax
import jax.numpy as jnp
from jax import lax
from jax.experimental import pallas as pl
from jax.experimental.pallas import tpu as pltpu

DSA_HEADS = 8
DSA_HEAD_DIM = 64
IDX_HEADS = 8
IDX_DIM = 64
TOPK_MAX = 256
DIFF_HEADS = 4
DIFF_HEAD_DIM = 64
DIFF_V_DIM = 2 * DIFF_HEAD_DIM
NUM_BUCKETS = 32
MAX_DISTANCE = 128
EPS = 1e-6

HEAD_COLS = 512
N_MAIN = 7 * HEAD_COLS
N_AUX = 128
ATT_BLOCK = 256
ROW_BLOCK = 512
IN_ROW_BLOCK = 1024

MASK_SCORE = -3.0e38
SELECT_ALL = -1.0e38
MASK_LOGIT = -1.0e30
LOG2E = math.log2(math.e)
SUM_ROWS = 16
ATT_GROUPS = 2
SEARCH_FIRST_STEPS = 18
SEARCH_EXTRA_ROUNDS = 11
VMEM_LIMIT = 56 * 1024 * 1024

F32 = jnp.float32
BF16 = jnp.bfloat16


def _inproj_kernel(x_ref, g_ref, w_ref, main_ref, aux_ref):
    x = x_ref[...]
    h = x * lax.rsqrt(jnp.mean(x * x, axis=-1, keepdims=True) + EPS) * g_ref[...]
    h = h.astype(BF16)
    for c in range(0, N_MAIN, HEAD_COLS):
        main_ref[:, c:c + HEAD_COLS] = jnp.dot(
            h, w_ref[:, c:c + HEAD_COLS], preferred_element_type=F32).astype(BF16)
    aux_ref[...] = jnp.dot(h, w_ref[:, N_MAIN:], preferred_element_type=F32)


def _inproj(x2, g, w_all):
    t, d = x2.shape
    rows = IN_ROW_BLOCK
    return pl.pallas_call(
        _inproj_kernel,
        grid=(t // rows,),
        in_specs=[
            pl.BlockSpec((rows, d), lambda i: (i, 0)),
            pl.BlockSpec((1, d), lambda i: (0, 0)),
            pl.BlockSpec((d, N_MAIN + N_AUX), lambda i: (0, 0)),
        ],
        out_specs=[
            pl.BlockSpec((rows, N_MAIN), lambda i: (i, 0)),
            pl.BlockSpec((rows, N_AUX), lambda i: (i, 0)),
        ],
        out_shape=[
            jax.ShapeDtypeStruct((t, N_MAIN), BF16),
            jax.ShapeDtypeStruct((t, N_AUX), F32),
        ],
        compiler_params=pltpu.CompilerParams(
            dimension_semantics=("arbitrary",), vmem_limit_bytes=VMEM_LIMIT),
        name="inproj",
    )(x2, g, w_all)


def _build_rel_bias(tab_ref, bias_near_ref, head0, n_heads):
    t = ATT_BLOCK
    key = lax.broadcasted_iota(jnp.int32, (t, t), 0)
    query = lax.broadcasted_iota(jnp.int32, (t, t), 1)
    max_exact = NUM_BUCKETS // 2
    for d in range(2):
        n = jnp.maximum(query - key + d * t, 0)
        nf = jnp.maximum(n, 1).astype(F32)
        large = max_exact + (jnp.log(nf / max_exact) / math.log(MAX_DISTANCE / max_exact)
                             * (NUM_BUCKETS - max_exact)).astype(jnp.int32)
        large = jnp.minimum(large, NUM_BUCKETS - 1)
        bucket = jnp.where(n < max_exact, n, large)
        for h in range(n_heads):
            bias_near_ref[d, h] = jnp.zeros((t, t), F32)

        def fill(b, carry, d=d, bucket=bucket):
            hit = bucket == b
            for h in range(n_heads):
                val = (tab_ref[b, head0 + h] - tab_ref[NUM_BUCKETS - 1, head0 + h]) * LOG2E
                bias_near_ref[d, h] = jnp.where(hit, val, bias_near_ref[d, h])
            return carry

        lax.fori_loop(0, NUM_BUCKETS - 1, fill, 0)


def _chunk(j):
    return pl.ds(pl.multiple_of(j * ATT_BLOCK, ATT_BLOCK), ATT_BLOCK)


def _half_lane_mask(half):
    lane = lax.broadcasted_iota(jnp.int32, (ATT_BLOCK, 128), 1)
    return (lane >= 64 * half) & (lane < 64 * (half + 1))


def _flash_update(m_ref, acc_ref, idx, s, vt):
    m = m_ref[idx]
    m_new = jnp.maximum(m, jnp.max(s, axis=0, keepdims=True))
    alpha = jnp.exp2(m - m_new)
    p = jnp.exp2(s - m_new)
    row = lax.broadcasted_iota(jnp.int32, (SUM_ROWS, vt.shape[1]), 0)
    ones_row = jnp.where(row == 0, 1.0, 0.0).astype(BF16)
    vt_sum = jnp.concatenate([vt, ones_row], axis=0)
    acc_ref[idx] = alpha * acc_ref[idx] + jnp.dot(vt_sum, p.astype(BF16),
                                                  preferred_element_type=F32)
    m_ref[idx] = m_new


def _flash_init(m_ref, acc_ref):
    m_ref[...] = jnp.full(m_ref.shape, MASK_LOGIT, F32)
    acc_ref[...] = jnp.zeros(acc_ref.shape, F32)


def _normalised(acc_ref, idx, dv):
    acc = acc_ref[idx]
    return acc[:dv] / acc[dv:dv + 1]


def _pipelined_steps(i, n_maps, logits, softmax_pv):
    per_group = n_maps // ATT_GROUPS
    groups = [range(g * per_group, (g + 1) * per_group) for g in range(ATT_GROUPS)]

    def step(j, carry, near=None, last=False):
        for g in range(ATT_GROUPS):
            if g + 1 < ATT_GROUPS:
                logits(j, groups[g + 1])
            elif not last:
                logits(j + 1, groups[0])
            softmax_pv(j, groups[g], near)
        return carry

    logits(0, groups[0])
    lax.fori_loop(0, i - 1, step, 0)
    lax.fori_loop(jnp.maximum(i - 1, 0), i, functools.partial(step, near=1), 0)
    step(i, 0, near=0, last=True)


def _dsa_kernel(n_sel, tab_ref, iq_ref, iwt_ref, ik_ref, q_ref, k_ref, vt_ref, out_ref,
                sc_ref, bias_ref, tri_ref, qz_ref, m_ref, acc_ref, s_ref):
    t = ATT_BLOCK
    i = pl.program_id(1)

    @pl.when((pl.program_id(0) == 0) & (i == 0))
    def _():
        _build_rel_bias(tab_ref, bias_ref, 0, DSA_HEADS)
        r = lax.broadcasted_iota(jnp.int32, (t, t), 0)
        c = lax.broadcasted_iota(jnp.int32, (t, t), 1)
        tri_ref[...] = jnp.where(c < r, 1.0, 0.0).astype(BF16)

    key = lax.broadcasted_iota(jnp.int32, (t, t), 0)
    query = lax.broadcasted_iota(jnp.int32, (t, t), 1)
    causal = key <= query
    nsel_f = jnp.float32(n_sel)
    big = jnp.float32(3.0e38)

    def load_masked_heads(src_ref):
        x = src_ref[0]
        for h in range(DSA_HEADS):
            pair = x[:, 128 * (h // 2):128 * (h // 2 + 1)]
            qz_ref[h] = jnp.where(_half_lane_mask(h % 2), pair, jnp.zeros_like(pair)).T

    load_masked_heads(iq_ref)
    w = iwt_ref[0] * (IDX_HEADS ** -0.5 * IDX_DIM ** -0.5)

    def score_tile(j):
        kc = ik_ref[0, _chunk(j), :]
        acc = jnp.zeros((t, t), F32)
        for h in range(IDX_HEADS):
            d = jnp.dot(kc, qz_ref[h], preferred_element_type=F32)
            acc = acc + w[h:h + 1, :] * jnp.maximum(d, 0.0)
        return acc

    def score_body(j, carry):
        lo, hi = carry
        acc = score_tile(j)
        sc_ref[_chunk(j), :] = acc
        return (jnp.minimum(lo, jnp.min(acc, axis=0, keepdims=True)),
                jnp.maximum(hi, jnp.max(acc, axis=0, keepdims=True)))

    lo, hi = lax.fori_loop(0, i, score_body,
                           (jnp.full((1, t), big, F32), jnp.full((1, t), -big, F32)))
    acc = score_tile(i)
    sc_ref[_chunk(i), :] = jnp.where(causal, acc, MASK_SCORE)
    lo = jnp.minimum(lo, jnp.min(jnp.where(causal, acc, big), axis=0, keepdims=True))
    hi = jnp.maximum(hi, jnp.max(jnp.where(causal, acc, -big), axis=0, keepdims=True))
    hi = hi + jnp.maximum(jnp.abs(hi), 1.0e-30)

    @pl.when(i % 2 == 0)
    def _():
        sc_ref[_chunk(i + 1), :] = jnp.full((t, t), MASK_SCORE, F32)

    n_span = i // 2 + 1

    def span(jj):
        return pl.ds(pl.multiple_of(jj * (2 * t), 2 * t), 2 * t)

    def fold(x):
        return x.reshape(x.shape[0] // 8, 8, t)

    def count(pred_fn):
        def body(jj, acc):
            for r in range(0, 2 * t, t // 2):
                rows = pl.ds(pl.multiple_of(jj * (2 * t) + r, t // 2), t // 2)
                ind = jnp.where(pred_fn(sc_ref[rows, :]), 1.0, 0.0)
                acc = acc + jnp.sum(fold(ind), axis=0)
            return acc
        part = lax.fori_loop(0, n_span, body, jnp.zeros((8, t), F32))
        return jnp.sum(part, axis=0, keepdims=True)

    def min_at_least(thr):
        def body(jj, acc):
            s = sc_ref[span(jj), :]
            return jnp.minimum(acc, jnp.min(fold(jnp.where(s >= thr, s, big)), axis=0))
        part = lax.fori_loop(0, n_span, body, jnp.full((8, t), big, F32))
        return jnp.min(part, axis=0, keepdims=True)

    pos = i * t + lax.broadcasted_iota(jnp.int32, (1, t), 1)
    small = pos < n_sel

    def count_pos_zero():
        def body(jj, acc):
            s = sc_ref[span(jj), :]
            return (acc[0] + jnp.sum(fold(jnp.where(s > 0.0, 1.0, 0.0)), axis=0),
                    acc[1] + jnp.sum(fold(jnp.where(s == 0.0, 1.0, 0.0)), axis=0))
        zero = jnp.zeros((8, t), F32)
        a, b = lax.fori_loop(0, n_span, body, (zero, zero))
        return jnp.sum(a, axis=0, keepdims=True), jnp.sum(b, axis=0, keepdims=True)

    n_pos, n_zero = count_pos_zero()
    zero_cut = (n_pos < nsel_f) & (n_pos + n_zero >= nsel_f) & jnp.logical_not(small)
    zflag = jnp.where(zero_cut, 1.0, 0.0)
    lo = jnp.where(n_pos >= nsel_f, 0.0, lo)
    hi = jnp.where(n_pos + n_zero < nsel_f, 0.0, hi)

    def bisect(st):
        lo, hi, thr, found = st
        mid = 0.5 * lo + 0.5 * hi
        cnt = count(lambda s: s >= mid)
        up = cnt >= nsel_f
        hit = cnt == nsel_f
        thr = jnp.where(hit & (found == 0.0), mid, thr)
        found = jnp.where(hit, 1.0, found)
        return (jnp.where(up, mid, lo), jnp.where(up, hi, mid), thr, found)

    def bisect_n(n, st):
        return lax.fori_loop(0, n, lambda _, s: bisect(s), st)

    def unresolved(flag):
        return jnp.min(flag) == 0.0

    def search_cond(st):
        it, _, _, _, found = st
        return (it < SEARCH_EXTRA_ROUNDS) & unresolved(jnp.maximum(found, zflag))

    def search_body(st):
        it, lo, hi, thr, found = st
        lo, hi, thr, found = bisect_n(2, (lo, hi, thr, found))
        return (it + 1, lo, hi, thr, found)

    thr0 = jnp.full((1, t), SELECT_ALL, F32)
    st = bisect_n(SEARCH_FIRST_STEPS, (lo, hi, thr0, jnp.where(small, 1.0, 0.0)))
    _, lo, hi, thr, found = lax.while_loop(search_cond, search_body, (jnp.int32(0),) + st)

    def tie_cond(st):
        _, _, _, found, _, above = st
        return unresolved(jnp.maximum(jnp.maximum(found, zflag),
                                      jnp.where(above < nsel_f, 1.0, 0.0)))

    def tie_body(st):
        lo, hi, thr, found, _, _ = st
        lo, hi, thr, found = bisect_n(4, (lo, hi, thr, found))
        tau = min_at_least(lo)
        return (lo, hi, thr, found, tau, count(lambda s: s > tau))

    _, _, thr, found, tau, above = lax.while_loop(
        tie_cond, tie_body,
        (lo, hi, thr, found, jnp.zeros((1, t), F32), jnp.full((1, t), nsel_f, F32)))

    no_tie = jnp.min(found) == 1.0

    @pl.when(no_tie)
    def _():
        def body(jj, carry):
            sc_ref[span(jj), :] = jnp.where(sc_ref[span(jj), :] >= thr, 0.0, MASK_LOGIT)
            return carry
        lax.fori_loop(0, n_span, body, 0)

    @pl.when(jnp.logical_not(no_tie))
    def _():
        found2 = found > 0.0
        cut = jnp.where(found2, thr, jnp.where(zero_cut, 0.0, tau))
        need = jnp.where(found2, jnp.float32(2 * ATT_BLOCK + 2),
                         nsel_f - jnp.where(zero_cut, n_pos, above))

        def body(jj, seen):
            s = sc_ref[span(jj), :]
            eq = s == cut
            eqf = jnp.where(eq, 1.0, 0.0)
            eqb = eqf.astype(BF16)
            first = jnp.sum(eqf[:t], axis=0, keepdims=True)
            rank = jnp.concatenate(
                [seen + jnp.dot(tri_ref[...], eqb[:t], preferred_element_type=F32),
                 seen + first + jnp.dot(tri_ref[...], eqb[t:], preferred_element_type=F32)],
                axis=0)
            keep = (s > cut) | (eq & (rank < need))
            sc_ref[span(jj), :] = jnp.where(keep, 0.0, MASK_LOGIT)
            total = first + jnp.sum(eqf[t:], axis=0, keepdims=True)
            return seen + jnp.where(found2, 0.0, total)
        lax.fori_loop(0, n_span, body, jnp.zeros((1, t), F32))

    load_masked_heads(q_ref)
    _flash_init(m_ref, acc_ref)

    def logits(j, heads):
        for h in heads:
            cols = slice(128 * (h // 2), 128 * (h // 2 + 1))
            s_ref[h] = jnp.dot(k_ref[0, _chunk(j), cols], qz_ref[h],
                               preferred_element_type=F32)

    def softmax_pv(j, heads, near):
        for h in heads:
            rows = slice(h * DSA_HEAD_DIM, (h + 1) * DSA_HEAD_DIM)
            s = s_ref[h] + sc_ref[_chunk(j), :]
            if near is not None:
                s = s + bias_ref[near, h]
            _flash_update(m_ref, acc_ref, h, s, vt_ref[0, rows, _chunk(j)])

    _pipelined_steps(i, DSA_HEADS, logits, softmax_pv)

    for p in range(DSA_HEADS // 2):
        o = jnp.concatenate([_normalised(acc_ref, h, DSA_HEAD_DIM)
                             for h in (2 * p, 2 * p + 1)], axis=0)
        out_ref[0, :, 128 * p:128 * (p + 1)] = o.T.astype(out_ref.dtype)


def _dsa(main, iwt, ik2, vt, tab, n_sel):
    b, s, _ = main.shape
    t = ATT_BLOCK
    assert s % (2 * t) == 0, "the score passes walk two key chunks at a time"
    v_rows = DSA_HEAD_DIM + SUM_ROWS
    col = lambda c: (lambda bi, i: (bi, i, c))
    full = lambda c: (lambda bi, i: (bi, 0, c))
    return pl.pallas_call(
        functools.partial(_dsa_kernel, n_sel),
        grid=(b, s // t),
        in_specs=[
            pl.BlockSpec(memory_space=pltpu.SMEM),
            pl.BlockSpec((1, t, HEAD_COLS), col(3)),
            pl.BlockSpec((1, IDX_HEADS, t), lambda bi, i: (bi, 0, i)),
            pl.BlockSpec((1, s, 2 * IDX_DIM), full(0)),
            pl.BlockSpec((1, t, HEAD_COLS), col(0)),
            pl.BlockSpec((1, s, HEAD_COLS), full(1)),
            pl.BlockSpec((1, HEAD_COLS, s), lambda bi, i: (bi, 0, 0)),
        ],
        out_specs=pl.BlockSpec((1, t, HEAD_COLS), col(0)),
        out_shape=jax.ShapeDtypeStruct((b, s, HEAD_COLS), BF16),
        scratch_shapes=[
            pltpu.VMEM((s, t), F32),
            pltpu.VMEM((2, DSA_HEADS, t, t), F32),
            pltpu.VMEM((t, t), BF16),
            pltpu.VMEM((DSA_HEADS, 128, t), BF16),
            pltpu.VMEM((DSA_HEADS, 1, t), F32),
            pltpu.VMEM((DSA_HEADS, v_rows, t), F32),
            pltpu.VMEM((DSA_HEADS, t, t), F32),
        ],
        compiler_params=pltpu.CompilerParams(
            dimension_semantics=("arbitrary", "arbitrary"), vmem_limit_bytes=VMEM_LIMIT),
        name="dsa",
    )(tab, main, iwt, ik2, main, main, vt)


def _diff_kernel(tab_ref, lam_ref, init_ref, subln_ref, q_ref, k_ref, vt_ref, out_ref,
                 bias_ref, qz_ref, m_ref, acc_ref, s_ref):
    t = ATT_BLOCK
    i = pl.program_id(1)
    n_maps = 2 * DIFF_HEADS

    @pl.when((pl.program_id(0) == 0) & (i == 0))
    def _():
        _build_rel_bias(tab_ref, bias_ref, DSA_HEADS, DIFF_HEADS)

    key = lax.broadcasted_iota(jnp.int32, (t, t), 0)
    query = lax.broadcasted_iota(jnp.int32, (t, t), 1)
    causal_bias = jnp.where(key <= query, 0.0, MASK_LOGIT)

    q = q_ref[0]
    for m in range(n_maps):
        pair = q[:, 128 * (m // 2):128 * (m // 2 + 1)]
        qz_ref[m] = jnp.where(_half_lane_mask(m % 2), pair, jnp.zeros_like(pair)).T

    lam = lam_ref[...]
    lam_init = init_ref[0:1, 0:1]
    lam_full = (jnp.exp(jnp.sum(lam[0:1] * lam[1:2], axis=-1, keepdims=True))
                - jnp.exp(jnp.sum(lam[2:3] * lam[3:4], axis=-1, keepdims=True)) + lam_init)

    _flash_init(m_ref, acc_ref)

    def logits(j, maps):
        for m in maps:
            cols = slice(128 * (m // 2), 128 * (m // 2 + 1))
            s_ref[m] = jnp.dot(k_ref[0, _chunk(j), cols], qz_ref[m],
                               preferred_element_type=F32)

    def softmax_pv(j, maps, near):
        for m in maps:
            rows = slice(DIFF_V_DIM * (m // 2), DIFF_V_DIM * (m // 2 + 1))
            s = s_ref[m]
            if near is not None:
                s = s + bias_ref[near, m // 2]
            if near == 0:
                s = s + causal_bias
            _flash_update(m_ref, acc_ref, m, s, vt_ref[0, rows, _chunk(j)])

    _pipelined_steps(i, n_maps, logits, softmax_pv)

    for h in range(DIFF_HEADS):
        cols = slice(128 * h, 128 * (h + 1))
        maps = [_normalised(acc_ref, 2 * h + c, DIFF_V_DIM) for c in range(2)]
        o = maps[0] - lam_full * maps[1]
        o = o * lax.rsqrt(jnp.mean(o * o, axis=0, keepdims=True) + EPS) * subln_ref[...]
        o = o * (1.0 - lam_init)
        out_ref[0, :, cols] = o.T.astype(out_ref.dtype)


def _diff(main, vt, tab, lam, init_row, subln_col):
    b, s, _ = main.shape
    t = ATT_BLOCK
    v_rows = DIFF_V_DIM + SUM_ROWS
    col = lambda c: (lambda bi, i: (bi, i, c))
    full = lambda c: (lambda bi, i: (bi, 0, c))
    const = lambda bi, i: (0, 0)
    return pl.pallas_call(
        _diff_kernel,
        grid=(b, s // t),
        in_specs=[
            pl.BlockSpec(memory_space=pltpu.SMEM),
            pl.BlockSpec((4, DIFF_HEAD_DIM), const),
            pl.BlockSpec((1, 128), const),
            pl.BlockSpec((DIFF_V_DIM, 1), const),
            pl.BlockSpec((1, t, HEAD_COLS), col(4)),
            pl.BlockSpec((1, s, HEAD_COLS), full(5)),
            pl.BlockSpec((1, HEAD_COLS, s), lambda bi, i: (bi, 0, 0)),
        ],
        out_specs=pl.BlockSpec((1, t, HEAD_COLS), col(0)),
        out_shape=jax.ShapeDtypeStruct((b, s, HEAD_COLS), BF16),
        scratch_shapes=[
            pltpu.VMEM((2, DIFF_HEADS, t, t), F32),
            pltpu.VMEM((2 * DIFF_HEADS, 128, t), BF16),
            pltpu.VMEM((2 * DIFF_HEADS, 1, t), F32),
            pltpu.VMEM((2 * DIFF_HEADS, v_rows, t), F32),
            pltpu.VMEM((2 * DIFF_HEADS, t, t), F32),
        ],
        compiler_params=pltpu.CompilerParams(
            dimension_semantics=("arbitrary", "arbitrary"), vmem_limit_bytes=VMEM_LIMIT),
        name="diff",
    )(tab, lam, init_row, subln_col, main, main, vt)


def _ffn_kernel(final, x_ref, a_ref, b_ref, wo_ref, g_ref, wg_ref, wu_ref, wd_ref, gf_ref,
                out_ref):
    half = wo_ref.shape[0] // 2
    x1 = (x_ref[...]
          + jnp.dot(a_ref[...], wo_ref[:half, :], preferred_element_type=F32)
          + jnp.dot(b_ref[...], wo_ref[half:, :], preferred_element_type=F32))
    h = x1 * lax.rsqrt(jnp.mean(x1 * x1, axis=-1, keepdims=True) + EPS) * g_ref[...]
    hb = h.astype(BF16)
    g = jnp.dot(hb, wg_ref[...], preferred_element_type=F32)
    u = jnp.dot(hb, wu_ref[...], preferred_element_type=F32)
    act = (g / (1.0 + jnp.exp(-g)) * u).astype(BF16)
    y = x1 + jnp.dot(act, wd_ref[...], preferred_element_type=F32)
    if final:
        y = y * lax.rsqrt(jnp.mean(y * y, axis=-1, keepdims=True) + EPS) * gf_ref[...]
    out_ref[...] = y


def _ffn(x2, a, b, wo, g, wg, wu, wd, gf, final):
    t, d = x2.shape
    d_ff = wg.shape[1]
    row = lambda i: (i, 0)
    const = lambda i: (0, 0)
    return pl.pallas_call(
        functools.partial(_ffn_kernel, final),
        grid=(t // ROW_BLOCK,),
        in_specs=[
            pl.BlockSpec((ROW_BLOCK, d), row),
            pl.BlockSpec((ROW_BLOCK, HEAD_COLS), row),
            pl.BlockSpec((ROW_BLOCK, HEAD_COLS), row),
            pl.BlockSpec((2 * HEAD_COLS, d), const),
            pl.BlockSpec((1, d), const),
            pl.BlockSpec((d, d_ff), const),
            pl.BlockSpec((d, d_ff), const),
            pl.BlockSpec((d_ff, d), const),
            pl.BlockSpec((1, d), const),
        ],
        out_specs=pl.BlockSpec((ROW_BLOCK, d), row),
        out_shape=jax.ShapeDtypeStruct((t, d), F32),
        compiler_params=pltpu.CompilerParams(
            dimension_semantics=("arbitrary",), vmem_limit_bytes=VMEM_LIMIT),
        name="ffn",
    )(x2, a, b, wo, g, wg, wu, wd, gf)


def _pack_w_in(w):
    hd = DSA_HEADS * DSA_HEAD_DIM
    off = 0
    parts = {}
    for name, width in (("dq", hd), ("dk", hd), ("dv", hd), ("iq", IDX_HEADS * IDX_DIM),
                        ("ik", IDX_DIM), ("iw", IDX_HEADS),
                        ("fq", 2 * DIFF_HEADS * DIFF_HEAD_DIM),
                        ("fk", 2 * DIFF_HEADS * DIFF_HEAD_DIM),
                        ("fv", DIFF_HEADS * DIFF_V_DIM)):
        parts[name] = w[:, off:off + width]
        off += width
    pad = jnp.zeros((w.shape[0], N_AUX - IDX_DIM - IDX_HEADS), w.dtype)
    return jnp.concatenate(
        [parts["dq"] * (DSA_HEAD_DIM ** -0.5 * LOG2E), parts["dk"], parts["dv"], parts["iq"],
         parts["fq"] * (DIFF_HEAD_DIM ** -0.5 * LOG2E), parts["fk"], parts["fv"],
         parts["ik"], parts["iw"], pad], axis=1).astype(BF16)


def kernel(x, attn_norm, w_in, diff_lambda, diff_subln, w_out, ffn_norm,
           w_gate, w_up, w_down, rel_bias, final_norm):
    b, s, d = x.shape
    depth = w_in.shape[0]
    n_sel = min(TOPK_MAX, s // 4)
    x2 = x.reshape(b * s, d)
    for l in range(depth):
        lam_init = 0.8 - 0.6 * math.exp(-0.3 * l)
        main, aux = _inproj(x2, attn_norm[l][None, :], _pack_w_in(w_in[l]))
        main = main.reshape(b, s, N_MAIN)
        aux = aux.reshape(b, s, N_AUX)
        ik = aux[:, :, :IDX_DIM].astype(BF16)
        ik2 = jnp.concatenate([ik, ik], axis=-1)
        iwt = jnp.swapaxes(aux[:, :, IDX_DIM:IDX_DIM + IDX_HEADS], 1, 2)
        dvt = jnp.swapaxes(main[:, :, 2 * HEAD_COLS:3 * HEAD_COLS], 1, 2)
        fvt = jnp.swapaxes(main[:, :, 6 * HEAD_COLS:7 * HEAD_COLS], 1, 2)
        dsa_out = _dsa(main, iwt, ik2, dvt, rel_bias, n_sel)
        diff_out = _diff(main, fvt, rel_bias, diff_lambda[l],
                         jnp.full((1, 128), lam_init, F32), diff_subln[l][:, None])
        x2 = _ffn(x2, dsa_out.reshape(b * s, HEAD_COLS), diff_out.reshape(b * s, HEAD_COLS),
                  w_out[l].astype(BF16), ffn_norm[l][None, :], w_gate[l].astype(BF16),
                  w_up[l].astype(BF16), w_down[l].astype(BF16), final_norm[None, :],
                  final=(l == depth - 1))
    return x2.reshape(b, s, d)
```

```python
import functools
import math

import jax
import jax.numpy as jnp
from jax import lax
from jax.experimental import pallas as pl
from jax.experimental.pallas import tpu as pltpu

DSA_HEADS = 8
DSA_HEAD_DIM = 64
IDX_HEADS = 8
IDX_DIM = 64
TOPK_MAX = 256
DIFF_HEADS = 4
DIFF_HEAD_DIM = 64
DIFF_V_DIM = 2 * DIFF_HEAD_DIM
NUM_BUCKETS = 32
MAX_DISTANCE = 128
EPS = 1e-6

HEAD_COLS = 512
N_MAIN = 7 * HEAD_COLS
N_AUX = 128
ATT_BLOCK = 256
ROW_BLOCK = 512
IN_ROW_BLOCK = 1024

MASK_SCORE = -3.0e38
SELECT_ALL = -1.0e38
MASK_LOGIT = -1.0e30
LOG2E = math.log2(math.e)
SUM_ROWS = 16
ATT_GROUPS = 2
SEARCH_FIRST_STEPS = 18
SEARCH_EXTRA_ROUNDS = 11
VMEM_LIMIT = 56 * 1024 * 1024

F32 = jnp.float32
BF16 = jnp.bfloat16


DV_GROUP, FV_GROUP = 2, 6


def _inproj_kernel(x_ref, g_ref, w_ref, main_ref, aux_ref, dvt_ref, fvt_ref):
    x = x_ref[...]
    h = x * lax.rsqrt(jnp.mean(x * x, axis=-1, keepdims=True) + EPS) * g_ref[...]
    h = h.astype(BF16)
    for grp in range(N_MAIN // HEAD_COLS):
        c = grp * HEAD_COLS
        res = jnp.dot(h, w_ref[:, c:c + HEAD_COLS], preferred_element_type=F32)
        main_ref[:, c:c + HEAD_COLS] = res.astype(BF16)
        if grp == DV_GROUP:
            dvt_ref[0] = res.T.astype(BF16)
        if grp == FV_GROUP:
            fvt_ref[0] = res.T.astype(BF16)
    aux_ref[...] = jnp.dot(h, w_ref[:, N_MAIN:], preferred_element_type=F32)


def _inproj(x2, g, w_all, seq):
    t, d = x2.shape
    rows = IN_ROW_BLOCK
    assert seq % rows == 0
    per_seq = seq // rows
    vt_spec = pl.BlockSpec((1, HEAD_COLS, rows), lambda i: (i // per_seq, 0, i % per_seq))
    vt_shape = jax.ShapeDtypeStruct((t // seq, HEAD_COLS, seq), BF16)
    return pl.pallas_call(
        _inproj_kernel,
        grid=(t // rows,),
        in_specs=[
            pl.BlockSpec((rows, d), lambda i: (i, 0)),
            pl.BlockSpec((1, d), lambda i: (0, 0)),
            pl.BlockSpec((d, N_MAIN + N_AUX), lambda i: (0, 0)),
        ],
        out_specs=[
            pl.BlockSpec((rows, N_MAIN), lambda i: (i, 0)),
            pl.BlockSpec((rows, N_AUX), lambda i: (i, 0)),
            vt_spec,
            vt_spec,
        ],
        out_shape=[
            jax.ShapeDtypeStruct((t, N_MAIN), BF16),
            jax.ShapeDtypeStruct((t, N_AUX), F32),
            vt_shape,
            vt_shape,
        ],
        compiler_params=pltpu.CompilerParams(
            dimension_semantics=("arbitrary",), vmem_limit_bytes=VMEM_LIMIT),
        name="inproj",
    )(x2, g, w_all)


def _build_rel_bias(tab_ref, bias_near_ref, head0, n_heads):
    t = ATT_BLOCK
    key = lax.broadcasted_iota(jnp.int32, (t, t), 0)
    query = lax.broadcasted_iota(jnp.int32, (t, t), 1)
    max_exact = NUM_BUCKETS // 2
    for d in range(2):
        n = jnp.maximum(query - key + d * t, 0)
        nf = jnp.maximum(n, 1).astype(F32)
        large = max_exact + (jnp.log(nf / max_exact) / math.log(MAX_DISTANCE / max_exact)
                             * (NUM_BUCKETS - max_exact)).astype(jnp.int32)
        large = jnp.minimum(large, NUM_BUCKETS - 1)
        bucket = jnp.where(n < max_exact, n, large)
        for h in range(n_heads):
            bias_near_ref[d, h] = jnp.zeros((t, t), F32)

        def fill(b, carry, d=d, bucket=bucket):
            hit = bucket == b
            for h in range(n_heads):
                val = (tab_ref[b, head0 + h] - tab_ref[NUM_BUCKETS - 1, head0 + h]) * LOG2E
                bias_near_ref[d, h] = jnp.where(hit, val, bias_near_ref[d, h])
            return carry

        lax.fori_loop(0, NUM_BUCKETS - 1, fill, 0)


def _chunk(j):
    return pl.ds(pl.multiple_of(j * ATT_BLOCK, ATT_BLOCK), ATT_BLOCK)


def _half_lane_mask(half):
    lane = lax.broadcasted_iota(jnp.int32, (ATT_BLOCK, 128), 1)
    return (lane >= 64 * half) & (lane < 64 * (half + 1))


def _flash_update(m_ref, acc_ref, idx, s, vt):
    m = m_ref[idx]
    m_new = jnp.maximum(m, jnp.max(s, axis=0, keepdims=True))
    alpha = jnp.exp2(m - m_new)
    p = jnp.exp2(s - m_new)
    row = lax.broadcasted_iota(jnp.int32, (SUM_ROWS, vt.shape[1]), 0)
    ones_row = jnp.where(row == 0, 1.0, 0.0).astype(BF16)
    vt_sum = jnp.concatenate([vt, ones_row], axis=0)
    acc_ref[idx] = alpha * acc_ref[idx] + jnp.dot(vt_sum, p.astype(BF16),
                                                  preferred_element_type=F32)
    m_ref[idx] = m_new


def _flash_init(m_ref, acc_ref):
    m_ref[...] = jnp.full(m_ref.shape, MASK_LOGIT, F32)
    acc_ref[...] = jnp.zeros(acc_ref.shape, F32)


def _normalised(acc_ref, idx, dv):
    acc = acc_ref[idx]
    return acc[:dv] / acc[dv:dv + 1]


def _pipelined_steps(i, n_maps, logits, softmax_pv):
    per_group = n_maps // ATT_GROUPS
    groups = [range(g * per_group, (g + 1) * per_group) for g in range(ATT_GROUPS)]

    def step(j, carry, near=None, last=False):
        for g in range(ATT_GROUPS):
            if g + 1 < ATT_GROUPS:
                logits(j, groups[g + 1])
            elif not last:
                logits(j + 1, groups[0])
            softmax_pv(j, groups[g], near)
        return carry

    logits(0, groups[0])
    lax.fori_loop(0, i - 1, step, 0)
    lax.fori_loop(jnp.maximum(i - 1, 0), i, functools.partial(step, near=1), 0)
    step(i, 0, near=0, last=True)


def _dsa_kernel(n_sel, tab_ref, iq_ref, iwt_ref, ik_ref, q_ref, k_ref, vt_ref, out_ref,
                sc_ref, bias_ref, tri_ref, qz_ref, m_ref, acc_ref, s_ref):
    t = ATT_BLOCK
    i = pl.program_id(1)

    @pl.when((pl.program_id(0) == 0) & (i == 0))
    def _():
        _build_rel_bias(tab_ref, bias_ref, 0, DSA_HEADS)
        r = lax.broadcasted_iota(jnp.int32, (t, t), 0)
        c = lax.broadcasted_iota(jnp.int32, (t, t), 1)
        tri_ref[...] = jnp.where(c < r, 1.0, 0.0).astype(BF16)

    key = lax.broadcasted_iota(jnp.int32, (t, t), 0)
    query = lax.broadcasted_iota(jnp.int32, (t, t), 1)
    causal = key <= query
    nsel_f = jnp.float32(n_sel)
    big = jnp.float32(3.0e38)

    def load_masked_heads(src_ref):
        x = src_ref[0]
        for h in range(DSA_HEADS):
            pair = x[:, 128 * (h // 2):128 * (h // 2 + 1)]
            qz_ref[h] = jnp.where(_half_lane_mask(h % 2), pair, jnp.zeros_like(pair)).T

    load_masked_heads(iq_ref)
    w = iwt_ref[0] * (IDX_HEADS ** -0.5 * IDX_DIM ** -0.5)

    def score_tile(j):
        kc = ik_ref[0, _chunk(j), :]
        acc = jnp.zeros((t, t), F32)
        for h in range(IDX_HEADS):
            d = jnp.dot(kc, qz_ref[h], preferred_element_type=F32)
            acc = acc + w[h:h + 1, :] * jnp.maximum(d, 0.0)
        return acc

    def score_body(j, carry):
        lo, hi = carry
        acc = score_tile(j)
        sc_ref[_chunk(j), :] = acc
        return (jnp.minimum(lo, jnp.min(acc, axis=0, keepdims=True)),
                jnp.maximum(hi, jnp.max(acc, axis=0, keepdims=True)))

    lo, hi = lax.fori_loop(0, i, score_body,
                           (jnp.full((1, t), big, F32), jnp.full((1, t), -big, F32)))
    acc = score_tile(i)
    sc_ref[_chunk(i), :] = jnp.where(causal, acc, MASK_SCORE)
    lo = jnp.minimum(lo, jnp.min(jnp.where(causal, acc, big), axis=0, keepdims=True))
    hi = jnp.maximum(hi, jnp.max(jnp.where(causal, acc, -big), axis=0, keepdims=True))
    hi = hi + jnp.maximum(jnp.abs(hi), 1.0e-30)

    @pl.when(i % 2 == 0)
    def _():
        sc_ref[_chunk(i + 1), :] = jnp.full((t, t), MASK_SCORE, F32)

    n_span = i // 2 + 1

    def span(jj):
        return pl.ds(pl.multiple_of(jj * (2 * t), 2 * t), 2 * t)

    def fold(x):
        return x.reshape(x.shape[0] // 8, 8, t)

    def count(pred_fn):
        def body(jj, acc):
            for r in range(0, 2 * t, t // 2):
                rows = pl.ds(pl.multiple_of(jj * (2 * t) + r, t // 2), t // 2)
                ind = jnp.where(pred_fn(sc_ref[rows, :]), 1.0, 0.0)
                acc = acc + jnp.sum(fold(ind), axis=0)
            return acc
        part = lax.fori_loop(0, n_span, body, jnp.zeros((8, t), F32))
        return jnp.sum(part, axis=0, keepdims=True)

    def min_at_least(thr):
        def body(jj, acc):
            s = sc_ref[span(jj), :]
            return jnp.minimum(acc, jnp.min(fold(jnp.where(s >= thr, s, big)), axis=0))
        part = lax.fori_loop(0, n_span, body, jnp.full((8, t), big, F32))
        return jnp.min(part, axis=0, keepdims=True)

    pos = i * t + lax.broadcasted_iota(jnp.int32, (1, t), 1)
    small = pos < n_sel

    def count_pos_zero():
        def body(jj, acc):
            s = sc_ref[span(jj), :]
            return (acc[0] + jnp.sum(fold(jnp.where(s > 0.0, 1.0, 0.0)), axis=0),
                    acc[1] + jnp.sum(fold(jnp.where(s == 0.0, 1.0, 0.0)), axis=0))
        zero = jnp.zeros((8, t), F32)
        a, b = lax.fori_loop(0, n_span, body, (zero, zero))
        return jnp.sum(a, axis=0, keepdims=True), jnp.sum(b, axis=0, keepdims=True)

    n_pos, n_zero = count_pos_zero()
    zero_cut = (n_pos < nsel_f) & (n_pos + n_zero >= nsel_f) & jnp.logical_not(small)
    zflag = jnp.where(zero_cut, 1.0, 0.0)
    lo = jnp.where(n_pos >= nsel_f, 0.0, lo)
    hi = jnp.where(n_pos + n_zero < nsel_f, 0.0, hi)

    def bisect(st):
        lo, hi, thr, found = st
        mid = 0.5 * lo + 0.5 * hi
        cnt = count(lambda s: s >= mid)
        up = cnt >= nsel_f
        hit = cnt == nsel_f
        thr = jnp.where(hit & (found == 0.0), mid, thr)
        found = jnp.where(hit, 1.0, found)
        return (jnp.where(up, mid, lo), jnp.where(up, hi, mid), thr, found)

    def bisect_n(n, st):
        return lax.fori_loop(0, n, lambda _, s: bisect(s), st)

    def unresolved(flag):
        return jnp.min(flag) == 0.0

    def search_cond(st):
        it, _, _, _, found = st
        return (it < SEARCH_EXTRA_ROUNDS) & unresolved(jnp.maximum(found, zflag))

    def search_body(st):
        it, lo, hi, thr, found = st
        lo, hi, thr, found = bisect_n(2, (lo, hi, thr, found))
        return (it + 1, lo, hi, thr, found)

    thr0 = jnp.full((1, t), SELECT_ALL, F32)
    st = bisect_n(SEARCH_FIRST_STEPS, (lo, hi, thr0, jnp.where(small, 1.0, 0.0)))
    _, lo, hi, thr, found = lax.while_loop(search_cond, search_body, (jnp.int32(0),) + st)

    def tie_cond(st):
        _, _, _, found, _, above = st
        return unresolved(jnp.maximum(jnp.maximum(found, zflag),
                                      jnp.where(above < nsel_f, 1.0, 0.0)))

    def tie_body(st):
        lo, hi, thr, found, _, _ = st
        lo, hi, thr, found = bisect_n(4, (lo, hi, thr, found))
        tau = min_at_least(lo)
        return (lo, hi, thr, found, tau, count(lambda s: s > tau))

    _, _, thr, found, tau, above = lax.while_loop(
        tie_cond, tie_body,
        (lo, hi, thr, found, jnp.zeros((1, t), F32), jnp.full((1, t), nsel_f, F32)))

    no_tie = jnp.min(found) == 1.0

    @pl.when(no_tie)
    def _():
        def body(jj, carry):
            sc_ref[span(jj), :] = jnp.where(sc_ref[span(jj), :] >= thr, 0.0, MASK_LOGIT)
            return carry
        lax.fori_loop(0, n_span, body, 0)

    @pl.when(jnp.logical_not(no_tie))
    def _():
        found2 = found > 0.0
        cut = jnp.where(found2, thr, jnp.where(zero_cut, 0.0, tau))
        need = jnp.where(found2, jnp.float32(2 * ATT_BLOCK + 2),
                         nsel_f - jnp.where(zero_cut, n_pos, above))

        def body(jj, seen):
            s = sc_ref[span(jj), :]
            eq = s == cut
            eqf = jnp.where(eq, 1.0, 0.0)
            eqb = eqf.astype(BF16)
            first = jnp.sum(eqf[:t], axis=0, keepdims=True)
            rank = jnp.concatenate(
                [seen + jnp.dot(tri_ref[...], eqb[:t], preferred_element_type=F32),
                 seen + first + jnp.dot(tri_ref[...], eqb[t:], preferred_element_type=F32)],
                axis=0)
            keep = (s > cut) | (eq & (rank < need))
            sc_ref[span(jj), :] = jnp.where(keep, 0.0, MASK_LOGIT)
            total = first + jnp.sum(eqf[t:], axis=0, keepdims=True)
            return seen + jnp.where(found2, 0.0, total)
        lax.fori_loop(0, n_span, body, jnp.zeros((1, t), F32))

    load_masked_heads(q_ref)
    _flash_init(m_ref, acc_ref)

    def logits(j, heads):
        for h in heads:
            cols = slice(128 * (h // 2), 128 * (h // 2 + 1))
            s_ref[h] = jnp.dot(k_ref[0, _chunk(j), cols], qz_ref[h],
                               preferred_element_type=F32)

    def softmax_pv(j, heads, near):
        for h in heads:
            rows = slice(h * DSA_HEAD_DIM, (h + 1) * DSA_HEAD_DIM)
            s = s_ref[h] + sc_ref[_chunk(j), :]
            if near is not None:
                s = s + bias_ref[near, h]
            _flash_update(m_ref, acc_ref, h, s, vt_ref[0, rows, _chunk(j)])

    _pipelined_steps(i, DSA_HEADS, logits, softmax_pv)

    for p in range(DSA_HEADS // 2):
        o = jnp.concatenate([_normalised(acc_ref, h, DSA_HEAD_DIM)
                             for h in (2 * p, 2 * p + 1)], axis=0)
        out_ref[0, :, 128 * p:128 * (p + 1)] = o.T.astype(out_ref.dtype)


def _dsa(main, iwt, ik2, vt, tab, n_sel):
    b, s, _ = main.shape
    t = ATT_BLOCK
    assert s % (2 * t) == 0, "the score passes walk two key chunks at a time"
    v_rows = DSA_HEAD_DIM + SUM_ROWS
    col = lambda c: (lambda bi, i: (bi, i, c))
    full = lambda c: (lambda bi, i: (bi, 0, c))
    return pl.pallas_call(
        functools.partial(_dsa_kernel, n_sel),
        grid=(b, s // t),
        in_specs=[
            pl.BlockSpec(memory_space=pltpu.SMEM),
            pl.BlockSpec((1, t, HEAD_COLS), col(3)),
            pl.BlockSpec((1, IDX_HEADS, t), lambda bi, i: (bi, 0, i)),
            pl.BlockSpec((1, s, 2 * IDX_DIM), full(0)),
            pl.BlockSpec((1, t, HEAD_COLS), col(0)),
            pl.BlockSpec((1, s, HEAD_COLS), full(1)),
            pl.BlockSpec((1, HEAD_COLS, s), lambda bi, i: (bi, 0, 0)),
        ],
        out_specs=pl.BlockSpec((1, t, HEAD_COLS), col(0)),
        out_shape=jax.ShapeDtypeStruct((b, s, HEAD_COLS), BF16),
        scratch_shapes=[
            pltpu.VMEM((s, t), F32),
            pltpu.VMEM((2, DSA_HEADS, t, t), F32),
            pltpu.VMEM((t, t), BF16),
            pltpu.VMEM((DSA_HEADS, 128, t), BF16),
            pltpu.VMEM((DSA_HEADS, 1, t), F32),
            pltpu.VMEM((DSA_HEADS, v_rows, t), F32),
            pltpu.VMEM((DSA_HEADS, t, t), F32),
        ],
        compiler_params=pltpu.CompilerParams(
            dimension_semantics=("arbitrary", "arbitrary"), vmem_limit_bytes=VMEM_LIMIT),
        name="dsa",
    )(tab, main, iwt, ik2, main, main, vt)


def _diff_kernel(tab_ref, lam_ref, init_ref, subln_ref, q_ref, k_ref, vt_ref, out_ref,
                 bias_ref, qz_ref, m_ref, acc_ref, s_ref):
    t = ATT_BLOCK
    i = pl.program_id(1)
    n_maps = 2 * DIFF_HEADS

    @pl.when((pl.program_id(0) == 0) & (i == 0))
    def _():
        _build_rel_bias(tab_ref, bias_ref, DSA_HEADS, DIFF_HEADS)

    key = lax.broadcasted_iota(jnp.int32, (t, t), 0)
    query = lax.broadcasted_iota(jnp.int32, (t, t), 1)
    causal_bias = jnp.where(key <= query, 0.0, MASK_LOGIT)

    q = q_ref[0]
    for m in range(n_maps):
        pair = q[:, 128 * (m // 2):128 * (m // 2 + 1)]
        qz_ref[m] = jnp.where(_half_lane_mask(m % 2), pair, jnp.zeros_like(pair)).T

    lam = lam_ref[...]
    lam_init = init_ref[0:1, 0:1]
    lam_full = (jnp.exp(jnp.sum(lam[0:1] * lam[1:2], axis=-1, keepdims=True))
                - jnp.exp(jnp.sum(lam[2:3] * lam[3:4], axis=-1, keepdims=True)) + lam_init)

    _flash_init(m_ref, acc_ref)

    def logits(j, maps):
        for m in maps:
            cols = slice(128 * (m // 2), 128 * (m // 2 + 1))
            s_ref[m] = jnp.dot(k_ref[0, _chunk(j), cols], qz_ref[m],
                               preferred_element_type=F32)

    def softmax_pv(j, maps, near):
        for m in maps:
            rows = slice(DIFF_V_DIM * (m // 2), DIFF_V_DIM * (m // 2 + 1))
            s = s_ref[m]
            if near is not None:
                s = s + bias_ref[near, m // 2]
            if near == 0:
                s = s + causal_bias
            _flash_update(m_ref, acc_ref, m, s, vt_ref[0, rows, _chunk(j)])

    _pipelined_steps(i, n_maps, logits, softmax_pv)

    for h in range(DIFF_HEADS):
        cols = slice(128 * h, 128 * (h + 1))
        maps = [_normalised(acc_ref, 2 * h + c, DIFF_V_DIM) for c in range(2)]
        o = maps[0] - lam_full * maps[1]
        o = o * lax.rsqrt(jnp.mean(o * o, axis=0, keepdims=True) + EPS) * subln_ref[...]
        o = o * (1.0 - lam_init)
        out_ref[0, :, cols] = o.T.astype(out_ref.dtype)


def _diff(main, vt, tab, lam, init_row, subln_col):
    b, s, _ = main.shape
    t = ATT_BLOCK
    v_rows = DIFF_V_DIM + SUM_ROWS
    col = lambda c: (lambda bi, i: (bi, i, c))
    full = lambda c: (lambda bi, i: (bi, 0, c))
    const = lambda bi, i: (0, 0)
    return pl.pallas_call(
        _diff_kernel,
        grid=(b, s // t),
        in_specs=[
            pl.BlockSpec(memory_space=pltpu.SMEM),
            pl.BlockSpec((4, DIFF_HEAD_DIM), const),
            pl.BlockSpec((1, 128), const),
            pl.BlockSpec((DIFF_V_DIM, 1), const),
            pl.BlockSpec((1, t, HEAD_COLS), col(4)),
            pl.BlockSpec((1, s, HEAD_COLS), full(5)),
            pl.BlockSpec((1, HEAD_COLS, s), lambda bi, i: (bi, 0, 0)),
        ],
        out_specs=pl.BlockSpec((1, t, HEAD_COLS), col(0)),
        out_shape=jax.ShapeDtypeStruct((b, s, HEAD_COLS), BF16),
        scratch_shapes=[
            pltpu.VMEM((2, DIFF_HEADS, t, t), F32),
            pltpu.VMEM((2 * DIFF_HEADS, 128, t), BF16),
            pltpu.VMEM((2 * DIFF_HEADS, 1, t), F32),
            pltpu.VMEM((2 * DIFF_HEADS, v_rows, t), F32),
            pltpu.VMEM((2 * DIFF_HEADS, t, t), F32),
        ],
        compiler_params=pltpu.CompilerParams(
            dimension_semantics=("arbitrary", "arbitrary"), vmem_limit_bytes=VMEM_LIMIT),
        name="diff",
    )(tab, lam, init_row, subln_col, main, main, vt)


def _ffn_kernel(final, x_ref, a_ref, b_ref, wo_ref, g_ref, wg_ref, wu_ref, wd_ref, gf_ref,
                out_ref):
    half = wo_ref.shape[0] // 2
    x1 = (x_ref[...]
          + jnp.dot(a_ref[...], wo_ref[:half, :], preferred_element_type=F32)
          + jnp.dot(b_ref[...], wo_ref[half:, :], preferred_element_type=F32))
    h = x1 * lax.rsqrt(jnp.mean(x1 * x1, axis=-1, keepdims=True) + EPS) * g_ref[...]
    hb = h.astype(BF16)
    g = jnp.dot(hb, wg_ref[...], preferred_element_type=F32)
    u = jnp.dot(hb, wu_ref[...], preferred_element_type=F32)
    act = (g / (1.0 + jnp.exp(-g)) * u).astype(BF16)
    y = x1 + jnp.dot(act, wd_ref[...], preferred_element_type=F32)
    if final:
        y = y * lax.rsqrt(jnp.mean(y * y, axis=-1, keepdims=True) + EPS) * gf_ref[...]
    out_ref[...] = y


def _ffn(x2, a, b, wo, g, wg, wu, wd, gf, final):
    t, d = x2.shape
    d_ff = wg.shape[1]
    row = lambda i: (i, 0)
    const = lambda i: (0, 0)
    return pl.pallas_call(
        functools.partial(_ffn_kernel, final),
        grid=(t // ROW_BLOCK,),
        in_specs=[
            pl.BlockSpec((ROW_BLOCK, d), row),
            pl.BlockSpec((ROW_BLOCK, HEAD_COLS), row),
            pl.BlockSpec((ROW_BLOCK, HEAD_COLS), row),
            pl.BlockSpec((2 * HEAD_COLS, d), const),
            pl.BlockSpec((1, d), const),
            pl.BlockSpec((d, d_ff), const),
            pl.BlockSpec((d, d_ff), const),
            pl.BlockSpec((d_ff, d), const),
            pl.BlockSpec((1, d), const),
        ],
        out_specs=pl.BlockSpec((ROW_BLOCK, d), row),
        out_shape=jax.ShapeDtypeStruct((t, d), F32),
        compiler_params=pltpu.CompilerParams(
            dimension_semantics=("arbitrary",), vmem_limit_bytes=VMEM_LIMIT),
        name="ffn",
    )(x2, a, b, wo, g, wg, wu, wd, gf)


def _pack_w_in(w):
    hd = DSA_HEADS * DSA_HEAD_DIM
    off = 0
    parts = {}
    for name, width in (("dq", hd), ("dk", hd), ("dv", hd), ("iq", IDX_HEADS * IDX_DIM),
                        ("ik", IDX_DIM), ("iw", IDX_HEADS),
                        ("fq", 2 * DIFF_HEADS * DIFF_HEAD_DIM),
                        ("fk", 2 * DIFF_HEADS * DIFF_HEAD_DIM),
                        ("fv", DIFF_HEADS * DIFF_V_DIM)):
        parts[name] = w[:, off:off + width]
        off += width
    pad = jnp.zeros((w.shape[0], N_AUX - IDX_DIM - IDX_HEADS), w.dtype)
    return jnp.concatenate(
        [parts["dq"] * (DSA_HEAD_DIM ** -0.5 * LOG2E), parts["dk"], parts["dv"], parts["iq"],
         parts["fq"] * (DIFF_HEAD_DIM ** -0.5 * LOG2E), parts["fk"], parts["fv"],
         parts["ik"], parts["iw"], pad], axis=1).astype(BF16)


def kernel(x, attn_norm, w_in, diff_lambda, diff_subln, w_out, ffn_norm,
           w_gate, w_up, w_down, rel_bias, final_norm):
    b, s, d = x.shape
    depth = w_in.shape[0]
    n_sel = min(TOPK_MAX, s // 4)
    x2 = x.reshape(b * s, d)
    for l in range(depth):
        lam_init = 0.8 - 0.6 * math.exp(-0.3 * l)
        main, aux, dvt, fvt = _inproj(x2, attn_norm[l][None, :], _pack_w_in(w_in[l]), s)
        main = main.reshape(b, s, N_MAIN)
        aux = aux.reshape(b, s, N_AUX)
        ik = aux[:, :, :IDX_DIM].astype(BF16)
        ik2 = jnp.concatenate([ik, ik], axis=-1)
        iwt = jnp.swapaxes(aux[:, :, IDX_DIM:IDX_DIM + IDX_HEADS], 1, 2)
        dsa_out = _dsa(main, iwt, ik2, dvt, rel_bias, n_sel)
        diff_out = _diff(main, fvt, rel_bias, diff_lambda[l],
                         jnp.full((1, 128), lam_init, F32), diff_subln[l][:, None])
        x2 = _ffn(x2, dsa_out.reshape(b * s, HEAD_COLS), diff_out.reshape(b * s, HEAD_COLS),
                  w_out[l].astype(BF16), ffn_norm[l][None, :], w_gate[l].astype(BF16),
                  w_up[l].astype(BF16), w_down[l].astype(BF16), final_norm[None, :],
                  final=(l == depth - 1))
    return x2.reshape(b, s, d)
```
